```python
import jax, jax.numpy as jnp
from jax import lax
import numpy as np

D_MODEL = 1024
BATCH = 8
SEQ = 2048
DEPTH = 1
DEC_BATCH = 8
DEC_SEQ = 16
PAST_LEN = 1024

CHUNK = 64
D_MIX = D_MODEL
D_CONV = D_MIX // 2
CONV_W = 3
N_HEADS = 8
HEAD_DIM = (D_MIX - D_CONV) // N_HEADS
N_KV_HEADS = 2
GROUP = N_HEADS // N_KV_HEADS
ROT_DIM = HEAD_DIM // 4
ROPE_THETA = 500000.0
N_IDX_HEADS = 8
IDX_DIM = 64
IDX_ROT_DIM = IDX_DIM // 4
TOP_K = 256
Q_BLOCK = 128
D_FF = 2816
EPS = 1e-6

SPLITS = (D_CONV, D_CONV, D_CONV,
          N_HEADS * HEAD_DIM, N_KV_HEADS * HEAD_DIM, N_KV_HEADS * HEAD_DIM,
          N_IDX_HEADS * IDX_DIM, IDX_DIM, N_IDX_HEADS)
D_IN = sum(SPLITS)
SPLIT_POINTS = tuple(int(s) for s in np.cumsum(SPLITS)[:-1])

kernel_name = 'hybrid_conv_dsa_macaron_stream_step'


def rmsnorm(x, g):
    xf = x.astype(jnp.float32)
    y = xf * lax.rsqrt(jnp.mean(xf * xf, axis=-1, keepdims=True) + EPS)
    return (y * g.astype(jnp.float32)).astype(x.dtype)


def swiglu(x, w_gate, w_up, w_down):
    return (jax.nn.silu(x @ w_gate) * (x @ w_up)) @ w_down


def rope_partial(x, pos, rot):
    half = rot // 2
    inv = ROPE_THETA ** (-jnp.arange(half, dtype=jnp.float32) * (2.0 / rot))
    ang = pos.astype(jnp.float32)[:, None] * inv[None, :]
    cos = jnp.cos(ang)[None, :, None, :]
    sin = jnp.sin(ang)[None, :, None, :]
    xr = x[..., :rot].astype(jnp.float32)
    x1, x2 = xr[..., :half], xr[..., half:]
    xr = jnp.concatenate([x1 * cos - x2 * sin, x1 * sin + x2 * cos], axis=-1).astype(x.dtype)
    return jnp.concatenate([xr, x[..., rot:]], axis=-1)


def dsa_attention(q, q_idx, w_idx, k, v, k_idx, q_pos, n_sel):
    B, T = q.shape[0], q.shape[1]
    S = k.shape[1]
    qb = Q_BLOCK if T % Q_BLOCK == 0 else T
    nb = T // qb
    key_pos = jnp.arange(S)

    def blocks(a):
        return jnp.moveaxis(a.reshape((B, nb, qb) + a.shape[2:]), 1, 0)

    def one_block(args):
        qq, qi, wi, pp = args
        limit = (pp // CHUNK + 1) * CHUNK
        adm = key_pos[None, :] < limit[:, None]
        dots = jnp.einsum('bthd,bsd->bths', qi.astype(jnp.float32), k_idx.astype(jnp.float32)) * (IDX_DIM ** -0.5)
        score = jnp.einsum('bths,bth->bts', jax.nn.relu(dots), wi.astype(jnp.float32))
        score = jnp.where(adm[None], score, -jnp.inf)
        _, sel = lax.top_k(score, n_sel)
        valid = sel < limit[None, :, None]
        k_sel = jax.vmap(lambda kb, ib: kb[ib])(k, sel)
        v_sel = jax.vmap(lambda vb, ib: vb[ib])(v, sel)
        qg = qq.reshape(B, qb, N_KV_HEADS, GROUP, HEAD_DIM)
        logits = jnp.einsum('btkgd,btnkd->btkgn', qg, k_sel).astype(jnp.float32) * (HEAD_DIM ** -0.5)
        logits = jnp.where(valid[:, :, None, None, :], logits, -jnp.inf)
        p = jax.nn.softmax(logits, axis=-1).astype(v.dtype)
        o = jnp.einsum('btkgn,btnkd->btkgd', p, v_sel)
        return o.reshape(B, qb, N_HEADS * HEAD_DIM)

    out = lax.map(one_block, (blocks(q), blocks(q_idx), blocks(w_idx), q_pos.reshape(nb, qb)))
    return jnp.moveaxis(out, 0, 1).reshape(B, T, N_HEADS * HEAD_DIM)


def trunk_layer(x, pos, k_past, v_past, idxk_past, conv_prev,
                ffn1_norm, ffn1_w_gate, ffn1_w_up, ffn1_w_down,
                mix_norm, w_in, conv_w, q_norm, k_norm, idx_k_norm, w_out,
                ffn2_norm, ffn2_w_gate, ffn2_w_up, ffn2_w_down):
    B, T, _ = x.shape
    x = x + 0.5 * swiglu(rmsnorm(x, ffn1_norm), ffn1_w_gate, ffn1_w_up, ffn1_w_down)
    h = rmsnorm(x, mix_norm)
    z = h @ w_in
    gb, gc, xv, q, k, v, qi, ki, wi = jnp.split(z, SPLIT_POINTS, axis=-1)
    u = gc * xv
    up = jnp.concatenate([conv_prev.astype(u.dtype), u], axis=1)
    conv = sum(up[:, j:j + T] * conv_w[j] for j in range(CONV_W))
    conv_out = gb * conv
    conv_state = up[:, up.shape[1] - (CONV_W - 1):]
    q = rope_partial(rmsnorm(q.reshape(B, T, N_HEADS, HEAD_DIM), q_norm), pos, ROT_DIM)
    k = rope_partial(rmsnorm(k.reshape(B, T, N_KV_HEADS, HEAD_DIM), k_norm), pos, ROT_DIM)
    v = v.reshape(B, T, N_KV_HEADS, HEAD_DIM)
    qi = rope_partial(qi.reshape(B, T, N_IDX_HEADS, IDX_DIM), pos, IDX_ROT_DIM)
    ki = rope_partial(rmsnorm(ki, idx_k_norm)[:, :, None, :], pos, IDX_ROT_DIM)[:, :, 0]
    wi = wi * (N_IDX_HEADS ** -0.5)
    k_all = jnp.concatenate([k_past.astype(k.dtype), k], axis=1)
    v_all = jnp.concatenate([v_past.astype(v.dtype), v], axis=1)
    ki_all = jnp.concatenate([idxk_past.astype(ki.dtype), ki], axis=1)
    L = k_all.shape[1]
    n_sel = max(1, min(TOP_K, L // 4))
    attn = dsa_attention(q, qi, wi, k_all, v_all, ki_all, pos, n_sel)
    x = x + jnp.concatenate([conv_out, attn], axis=-1) @ w_out
    x = x + 0.5 * swiglu(rmsnorm(x, ffn2_norm), ffn2_w_gate, ffn2_w_up, ffn2_w_down)
    return x, k, v, ki, conv_state


def setup_inputs(seed: int = 0) -> dict:
    key = jax.random.key(seed)
    ks = jax.random.split(key, 24)
    f32 = jnp.float32
    nrm = lambda k, shape, s: jax.random.normal(k, shape, f32) * s
    gain = lambda k, n: 1.0 + nrm(k, (DEPTH, n), 0.02)
    return {
        'x_prompt': nrm(ks[0], (BATCH, SEQ, D_MODEL), 1.0),
        'x_sample': nrm(ks[1], (DEC_BATCH, DEC_SEQ, D_MODEL), 1.0),
        'cache_k': nrm(ks[2], (DEPTH, DEC_BATCH, PAST_LEN, N_KV_HEADS, HEAD_DIM), 1.0),
        'cache_v': nrm(ks[3], (DEPTH, DEC_BATCH, PAST_LEN, N_KV_HEADS, HEAD_DIM), 1.0),
        'cache_idx_k': nrm(ks[4], (DEPTH, DEC_BATCH, PAST_LEN, IDX_DIM), 1.0),
        'state_conv': nrm(ks[5], (DEPTH, DEC_BATCH, CONV_W - 1, D_CONV), 0.5),
        'ffn1_norm': gain(ks[6], D_MODEL),
        'ffn1_w_gate': nrm(ks[7], (DEPTH, D_MODEL, D_FF), D_MODEL ** -0.5),
        'ffn1_w_up': nrm(ks[8], (DEPTH, D_MODEL, D_FF), D_MODEL ** -0.5),
        'ffn1_w_down': nrm(ks[9], (DEPTH, D_FF, D_MODEL), D_FF ** -0.5),
        'mix_norm': gain(ks[10], D_MODEL),
        'w_in': nrm(ks[11], (DEPTH, D_MODEL, D_IN), D_MODEL ** -0.5),
        'conv_w': nrm(ks[12], (DEPTH, CONV_W, D_CONV), CONV_W ** -0.5),
        'q_norm': gain(ks[13], HEAD_DIM),
        'k_norm': gain(ks[14], HEAD_DIM),
        'idx_k_norm': gain(ks[15], IDX_DIM),
        'w_out': nrm(ks[16], (DEPTH, D_MIX, D_MODEL), D_MIX ** -0.5),
        'ffn2_norm': gain(ks[17], D_MODEL),
        'ffn2_w_gate': nrm(ks[18], (DEPTH, D_MODEL, D_FF), D_MODEL ** -0.5),
        'ffn2_w_up': nrm(ks[19], (DEPTH, D_MODEL, D_FF), D_MODEL ** -0.5),
        'ffn2_w_down': nrm(ks[20], (DEPTH, D_FF, D_MODEL), D_FF ** -0.5),
    }


def reference(x_prompt, x_sample, cache_k, cache_v, cache_idx_k, state_conv,
              ffn1_norm, ffn1_w_gate, ffn1_w_up, ffn1_w_down,
              mix_norm, w_in, conv_w, q_norm, k_norm, idx_k_norm, w_out,
              ffn2_norm, ffn2_w_gate, ffn2_w_up, ffn2_w_down):
    Bp, Tp = x_prompt.shape[0], x_prompt.shape[1]
    Bs, Ts = x_sample.shape[0], x_sample.shape[1]
    past = cache_k.shape[2]
    pos_p = jnp.arange(Tp)
    pos_s = past + jnp.arange(Ts)
    dt = x_prompt.dtype
    empty_kv = jnp.zeros((Bp, 0, N_KV_HEADS, HEAD_DIM), dt)
    empty_ik = jnp.zeros((Bp, 0, IDX_DIM), dt)
    zero_conv = jnp.zeros((Bp, CONV_W - 1, D_CONV), dt)
    hp, hs = x_prompt, x_sample
    kp_l, vp_l, ip_l, cp_l, ks_l, vs_l, is_l, cs_l = [], [], [], [], [], [], [], []
    for l in range(DEPTH):
        w = (ffn1_norm[l], ffn1_w_gate[l], ffn1_w_up[l], ffn1_w_down[l],
             mix_norm[l], w_in[l], conv_w[l], q_norm[l], k_norm[l], idx_k_norm[l], w_out[l],
             ffn2_norm[l], ffn2_w_gate[l], ffn2_w_up[l], ffn2_w_down[l])
        hp, kp, vp, ip, cp = trunk_layer(hp, pos_p, empty_kv, empty_kv, empty_ik, zero_conv, *w)
        hs, ks_, vs, is_, cs = trunk_layer(hs, pos_s, cache_k[l], cache_v[l], cache_idx_k[l], state_conv[l], *w)
        kp_l.append(kp); vp_l.append(vp); ip_l.append(ip); cp_l.append(cp)
        ks_l.append(ks_); vs_l.append(vs); is_l.append(is_); cs_l.append(cs)
    k_prompt = jnp.stack(kp_l); v_prompt = jnp.stack(vp_l)
    idx_k_prompt = jnp.stack(ip_l); conv_prompt = jnp.stack(cp_l)
    k_sample = jnp.stack(ks_l); v_sample = jnp.stack(vs_l)
    idx_k_sample = jnp.stack(is_l); conv_sample = jnp.stack(cs_l)
    return (hp, hs, k_prompt, v_prompt, idx_k_prompt, conv_prompt, k_sample, v_sample, idx_k_sample, conv_sample)
```

```python
import functools

import numpy as np
import jax
import jax.numpy as jnp
from jax import lax
from jax.experimental import pallas as pl
from jax.experimental.pallas import tpu as pltpu

F32 = jnp.float32
BF16 = jnp.bfloat16
I32 = jnp.int32

CHUNK = 64
CONV_W = 3
N_HEADS = 8
N_KV_HEADS = 2
HEAD_DIM = 64
ROT_DIM = 16
ROPE_THETA = 500000.0
N_IDX_HEADS = 8
IDX_DIM = 64
TOP_K = 256
EPS = 1e-6

LANES = 128
SUBLANES = 8
INT_MIN = -(2 ** 31)
NEG_BIG = -1e30
VMEM_LIMIT = 56 * 1024 * 1024


def _const_spec(shape):
    nd = len(shape)
    return pl.BlockSpec(shape, lambda *_: (0,) * nd, pipeline_mode=pl.Buffered(1))


def _dot(a, b):
    return jnp.dot(a, b, preferred_element_type=F32)


def _dot_nt(a, b):
    return lax.dot_general(a, b, (((1,), (1,)), ((), ())), preferred_element_type=F32)


def _split_bf16(x):
    hi = x.astype(BF16).astype(F32)
    lo = (x - hi).astype(BF16).astype(F32)
    return hi, lo


def _rmsnorm(x, g):
    ms = jnp.mean(x * x, axis=-1, keepdims=True)
    return (x * lax.rsqrt(ms + EPS)) * g


def _swiglu(h, wg_ref, wu_ref, wd_ref):
    g = _dot(h, wg_ref[...])
    u = _dot(h, wu_ref[...])
    a = (g * jax.nn.sigmoid(g)) * u
    return _dot(a.astype(BF16), wd_ref[...])


def _ffn_kernel(x_ref, g_ref, wg_ref, wu_ref, wd_ref, o_ref):
    x = x_ref[...]
    h = _rmsnorm(x, g_ref[...]).astype(BF16)
    o_ref[...] = x + 0.5 * _swiglu(h, wg_ref, wu_ref, wd_ref)


def _ffn_call(x, g, wg, wu, wd, tm):
    n, d = x.shape
    dff = wg.shape[1]
    return pl.pallas_call(
        _ffn_kernel,
        grid=(n // tm,),
        in_specs=[
            pl.BlockSpec((tm, d), lambda i: (i, 0)),
            _const_spec((1, d)),
            _const_spec((d, dff)),
            _const_spec((d, dff)),
            _const_spec((dff, d)),
        ],
        out_specs=pl.BlockSpec((tm, d), lambda i: (i, 0)),
        out_shape=jax.ShapeDtypeStruct((n, d), F32),
        compiler_params=pltpu.CompilerParams(
            dimension_semantics=("arbitrary",), vmem_limit_bytes=VMEM_LIMIT),
        name="ffn",
    )(x, g, wg, wu, wd)


def _out_ffn_kernel(x_ref, c_ref, a_ref, woc_ref, woa_ref, g_ref, wg_ref, wu_ref, wd_ref, o_ref):
    x = x_ref[...] + (_dot(c_ref[...], woc_ref[...]) + _dot(a_ref[...], woa_ref[...]))
    h = _rmsnorm(x, g_ref[...]).astype(BF16)
    o_ref[...] = x + 0.5 * _swiglu(h, wg_ref, wu_ref, wd_ref)


def _out_ffn_call(x, conv_out, attn, woc, woa, g, wg, wu, wd, tm):
    n, d = x.shape
    dff = wg.shape[1]
    dc = conv_out.shape[1]
    da = attn.shape[1]
    return pl.pallas_call(
        _out_ffn_kernel,
        grid=(n // tm,),
        in_specs=[
            pl.BlockSpec((tm, d), lambda i: (i, 0)),
            pl.BlockSpec((tm, dc), lambda i: (i, 0)),
            pl.BlockSpec((tm, da), lambda i: (i, 0)),
            _const_spec((dc, d)),
            _const_spec((da, d)),
            _const_spec((1, d)),
            _const_spec((d, dff)),
            _const_spec((d, dff)),
            _const_spec((dff, d)),
        ],
        out_specs=pl.BlockSpec((tm, d), lambda i: (i, 0)),
        out_shape=jax.ShapeDtypeStruct((n, d), F32),
        compiler_params=pltpu.CompilerParams(
            dimension_semantics=("arbitrary",), vmem_limit_bytes=VMEM_LIMIT),
        name="out_ffn",
    )(x, conv_out, attn, woc, woa, g, wg, wu, wd)


def _rope(y, cos, sin_lo, sin_hi):
    half = ROT_DIM // 2
    return (y * cos + pltpu.roll(y, LANES - half, 1) * sin_lo) + pltpu.roll(y, half, 1) * sin_hi


def _group_sumsq(x, bd):
    hi, lo = _split_bf16(x * x)
    return _dot(hi.astype(BF16), bd) + _dot(lo.astype(BF16), bd)


def _head_norm(x, g, bd):
    ms = _group_sumsq(x, bd) * (1.0 / HEAD_DIM)
    return (x * lax.rsqrt(ms + EPS)) * g


def _stack_keys(kin):
    lane = lax.broadcasted_iota(I32, kin.shape, 1)
    first = lane < IDX_DIM
    hi, lo = _split_bf16(kin)
    c0 = jnp.where(first, hi, pltpu.roll(hi, IDX_DIM, 1))
    c1 = jnp.where(first, lo, 0.0)
    return jnp.concatenate([c0, c1], axis=1).astype(BF16)


def _stack_queries(qc):
    lane = lax.broadcasted_iota(I32, qc.shape, 1)
    first = lane < IDX_DIM
    hi, lo = _split_bf16(qc)
    hi_sw = pltpu.roll(hi, IDX_DIM, 1)
    lo_sw = pltpu.roll(lo, IDX_DIM, 1)
    a0 = jnp.where(first, hi, lo_sw)
    a1 = jnp.where(first, hi, 0.0)
    b0 = jnp.where(first, hi_sw, lo)
    b1 = jnp.where(first, hi_sw, 0.0)
    return jnp.concatenate([a0, a1, b0, b1], axis=1).astype(BF16)


def _proj_kernel(x_ref, gmix_ref, wmain_ref, wih_ref, wil_ref, convw_ref, cprev_ref,
                 gq_ref, gk_ref, gik_ref, cos_ref, slo_ref, shi_ref, bd_ref,
                 convout_ref, q_ref, k_ref, v_ref, kb_ref, qi3_ref, kiwi_ref, ki3_ref, ulast_ref,
                 ubuf_ref, *, tiles_per_batch, d_conv):
    i = pl.program_id(0)
    tm = x_ref.shape[0]
    h = _rmsnorm(x_ref[...], gmix_ref[...])
    h_hi = h.astype(BF16)
    h_lo = (h - h_hi.astype(F32)).astype(BF16)
    zm = _dot(h_hi, wmain_ref[...])
    wih = wih_ref[...]
    zi = (_dot(h_hi, wih) + _dot(h_lo, wih)) + _dot(h_hi, wil_ref[...])

    gb = zm[:, 0:d_conv]
    u = zm[:, d_conv:2 * d_conv] * zm[:, 2 * d_conv:3 * d_conv]

    @pl.when(i % tiles_per_batch == 0)
    def _():
        ubuf_ref[0:SUBLANES, :] = cprev_ref[0]

    ubuf_ref[SUBLANES:SUBLANES + tm, :] = u
    w = convw_ref[...]
    conv = (ubuf_ref[SUBLANES - 2:SUBLANES - 2 + tm, :] * w[0:1, :]
            + ubuf_ref[SUBLANES - 1:SUBLANES - 1 + tm, :] * w[1:2, :]) + u * w[2:3, :]
    convout_ref[...] = (gb * conv).astype(BF16)
    tail = u[tm - SUBLANES:tm, :]
    ulast_ref[0] = tail
    ubuf_ref[0:SUBLANES, :] = tail

    cos = cos_ref[...]
    slo = slo_ref[...]
    shi = shi_ref[...]
    bd = bd_ref[...]
    q0 = 3 * d_conv
    nq = N_HEADS * HEAD_DIM
    scale = HEAD_DIM ** -0.5
    for c in range(nq // LANES):
        qc = zm[:, q0 + c * LANES:q0 + (c + 1) * LANES]
        qn = _head_norm(qc, gq_ref[...], bd)
        q_ref[:, c * LANES:(c + 1) * LANES] = (_rope(qn, cos, slo, shi) * scale).astype(BF16)
    k0 = q0 + nq
    kn = _rope(_head_norm(zm[:, k0:k0 + LANES], gk_ref[...], bd), cos, slo, shi)
    k_ref[...] = kn
    kb_ref[...] = kn.astype(BF16)
    v_ref[...] = zm[:, k0 + LANES:k0 + 2 * LANES]

    ni = N_IDX_HEADS * IDX_DIM
    for c in range(ni // LANES):
        qic = _rope(zi[:, c * LANES:(c + 1) * LANES], cos, slo, shi)
        qi3_ref[:, c * 4 * LANES:(c + 1) * 4 * LANES] = _stack_queries(qic)
    kw = zi[:, ni:ni + LANES]
    kin = _rope(_head_norm(kw, gik_ref[...], bd), cos, slo, shi)
    lane = lax.broadcasted_iota(I32, kw.shape, 1)
    kiwi_ref[...] = jnp.where(lane < IDX_DIM, kin, kw * (N_IDX_HEADS ** -0.5))
    ki3_ref[...] = _stack_keys(kin)


def _proj_call(x1, cprev, wts, tables, tm, tiles_per_batch):
    n, d = x1.shape
    (gmix, wmain, wih, wil, convw, gq, gk, gik, bd) = wts
    cos, slo, shi = tables
    d_conv = convw.shape[1]
    nt = n // tm
    row = lambda w: pl.BlockSpec((tm, w), lambda i: (i, 0))
    tab = pl.BlockSpec((tm, LANES), lambda i: (i % tiles_per_batch, 0))
    nq = N_HEADS * HEAD_DIM
    out_shape = (
        jax.ShapeDtypeStruct((n, d_conv), BF16),
        jax.ShapeDtypeStruct((n, nq), BF16),
        jax.ShapeDtypeStruct((n, LANES), F32),
        jax.ShapeDtypeStruct((n, LANES), F32),
        jax.ShapeDtypeStruct((n, LANES), BF16),
        jax.ShapeDtypeStruct((n, N_IDX_HEADS * 2 * LANES), BF16),
        jax.ShapeDtypeStruct((n, LANES), F32),
        jax.ShapeDtypeStruct((n, 2 * LANES), BF16),
        jax.ShapeDtypeStruct((nt, SUBLANES, d_conv), F32),
    )
    out_specs = (
        row(d_conv), row(nq), row(LANES), row(LANES), row(LANES), row(N_IDX_HEADS * 2 * LANES),
        row(LANES), row(2 * LANES),
        pl.BlockSpec((1, SUBLANES, d_conv), lambda i: (i, 0, 0)),
    )
    in_specs = [
        row(d),
        _const_spec(gmix.shape), _const_spec(wmain.shape), _const_spec(wih.shape), _const_spec(wil.shape),
        _const_spec(convw.shape),
        pl.BlockSpec((1, SUBLANES, d_conv), lambda i: (i // tiles_per_batch, 0, 0)),
        _const_spec(gq.shape), _const_spec(gk.shape), _const_spec(gik.shape),
        tab, tab, tab,
        _const_spec(bd.shape),
    ]
    return pl.pallas_call(
        functools.partial(_proj_kernel, tiles_per_batch=tiles_per_batch, d_conv=d_conv),
        grid=(nt,),
        in_specs=in_specs,
        out_specs=out_specs,
        out_shape=out_shape,
        scratch_shapes=[pltpu.VMEM((tm + SUBLANES, d_conv), F32)],
        compiler_params=pltpu.CompilerParams(
            dimension_semantics=("arbitrary",), vmem_limit_bytes=VMEM_LIMIT),
        name="proj",
    )(x1, gmix, wmain, wih, wil, convw, cprev, gq, gk, gik, cos, slo, shi, bd)


def _stack_keys_kernel(k_ref, o_ref):
    o_ref[...] = _stack_keys(k_ref[...])


def _stack_keys_call(kpad, tm):
    n = kpad.shape[0]
    return pl.pallas_call(
        _stack_keys_kernel,
        grid=(n // tm,),
        in_specs=[pl.BlockSpec((tm, LANES), lambda i: (i, 0))],
        out_specs=pl.BlockSpec((tm, 2 * LANES), lambda i: (i, 0)),
        out_shape=jax.ShapeDtypeStruct((n, 2 * LANES), BF16),
        name="stack_keys",
    )(kpad)


def _attn_kernel(qi3_ref, wit_ref, q_ref, ki3_ref, kb_ref, vt_ref, o_ref,
                 key_ref, bias_ref, ot_ref, *, kc, past, n_keys, n_sel, idx_bits):
    tq = q_ref.shape[0]
    j = pl.program_id(1)
    q_first = past + j * tq
    pos = q_first + lax.broadcasted_iota(I32, (1, tq), 1)
    limit = jnp.minimum(((pos >> 6) + 1) * CHUNK, n_keys)
    max_limit = jnp.minimum((((q_first + tq - 1) >> 6) + 1) * CHUNK, n_keys)
    nk = (max_limit + kc - 1) // kc

    def rows(c):
        return pl.ds(pl.multiple_of(c * kc, kc), kc)

    def key_pos(c):
        return c * kc + lax.broadcasted_iota(I32, (kc, tq), 0)

    w8 = wit_ref[0] * (IDX_DIM ** -0.5)

    def score_body(c, carry):
        kk = ki3_ref[rows(c), :]
        acc = jnp.zeros((kc, tq), F32)
        for hh in range(N_IDX_HEADS):
            d = _dot_nt(kk, qi3_ref[:, hh * 2 * LANES:(hh + 1) * 2 * LANES])
            acc = acc + jnp.maximum(d, 0.0) * w8[hh:hh + 1, :]
        bits = pltpu.bitcast(acc + 0.0, I32)
        key = bits ^ ((bits >> 31) & 0x7FFFFFFF)
        key_ref[rows(c), :] = jnp.where(key_pos(c) < limit, key, INT_MIN)
        return carry

    lax.fori_loop(0, nk, score_body, 0)

    def count(pred):
        def body(c, acc):
            m = pred(c, key_ref[rows(c), :]).astype(I32)
            return acc + m.reshape(kc // SUBLANES, SUBLANES, tq).sum(axis=0)
        acc = lax.fori_loop(0, nk, body, jnp.zeros((SUBLANES, tq), I32))
        return acc.sum(axis=0, keepdims=True)

    c_nonneg = count(lambda c, k: k >= 0)
    thr0 = jnp.where(c_nonneg >= n_sel, 0, INT_MIN).astype(I32)

    def bit_body(b, thr):
        cand = thr | (jnp.int32(1) << (30 - b))
        cnt = count(lambda c, k: k >= cand)
        return jnp.where(cnt >= n_sel, cand, thr)

    thr = lax.fori_loop(0, 31, bit_body, thr0)

    c_gt = count(lambda c, k: k > thr)
    c_ge = count(lambda c, k: k >= thr)
    room = n_sel - c_gt
    live = thr != INT_MIN
    surplus = jnp.max(jnp.where(live & (c_ge > n_sel), 1, 0))

    def tie_search():
        def body(b, jm):
            cand = jm | (jnp.int32(1) << (idx_bits - 1 - b))
            cnt = count(lambda c, k: (k == thr) & (key_pos(c) < cand))
            return jnp.where(cnt < room, cand, jm)
        return lax.fori_loop(0, idx_bits, body, jnp.zeros((1, tq), I32))

    jm = lax.cond(surplus > 0, tie_search, lambda: jnp.full((1, tq), 2 ** idx_bits, I32))
    jm = jnp.where(live, jm, -1)

    def bias_body(c, carry):
        k = key_ref[rows(c), :]
        sel = (k > thr) | ((k == thr) & (key_pos(c) <= jm))
        bias_ref[rows(c), :] = jnp.where(sel, 0.0, NEG_BIG)
        return carry

    lax.fori_loop(0, nk, bias_body, 0)

    group = N_HEADS // N_KV_HEADS
    for hh in range(N_HEADS):
        kv = hh // group
        qh = q_ref[:, hh * HEAD_DIM:(hh + 1) * HEAD_DIM]

        def att_body(c, carry, kv=kv, qh=qh):
            m, l, acc = carry
            kk = kb_ref[rows(c), kv * HEAD_DIM:(kv + 1) * HEAD_DIM]
            s = _dot_nt(kk, qh) + bias_ref[rows(c), :]
            m_new = jnp.maximum(m, s.max(axis=0, keepdims=True))
            alpha = jnp.exp(m - m_new)
            p = jnp.exp(s - m_new)
            l = alpha * l + p.sum(axis=0, keepdims=True)
            vt = vt_ref[c, kv * HEAD_DIM:(kv + 1) * HEAD_DIM, :]
            acc = alpha * acc + _dot(vt, p.astype(BF16))
            return m_new, l, acc

        init = (jnp.full((1, tq), NEG_BIG, F32), jnp.zeros((1, tq), F32), jnp.zeros((HEAD_DIM, tq), F32))
        _, l, acc = lax.fori_loop(0, nk, att_body, init)
        ot_ref[hh * HEAD_DIM:(hh + 1) * HEAD_DIM, :] = acc / l

    o_ref[...] = ot_ref[...].T.astype(BF16)


def _attn_call(qi3, wit, q, ki3, kb, vt, *, batch, tq, kc, past, n_keys, n_sel):
    tq_total = q.shape[0] // batch
    nq = tq_total // tq
    lp = ki3.shape[1]
    idx_bits = max(1, int(np.ceil(np.log2(lp))))
    d_attn = q.shape[1]
    kernel = functools.partial(_attn_kernel, kc=kc, past=past, n_keys=n_keys, n_sel=n_sel, idx_bits=idx_bits)
    return pl.pallas_call(
        kernel,
        grid=(batch, nq),
        in_specs=[
            pl.BlockSpec((tq, qi3.shape[1]), lambda b, j: (b * nq + j, 0)),
            pl.BlockSpec((1, N_IDX_HEADS, tq), lambda b, j: (b, 0, j)),
            pl.BlockSpec((tq, d_attn), lambda b, j: (b * nq + j, 0)),
            pl.BlockSpec((None, lp, ki3.shape[2]), lambda b, j: (b, 0, 0)),
            pl.BlockSpec((None, lp, kb.shape[2]), lambda b, j: (b, 0, 0)),
            pl.BlockSpec((None, lp // kc, vt.shape[2], kc), lambda b, j: (b, 0, 0, 0)),
        ],
        out_specs=pl.BlockSpec((tq, d_attn), lambda b, j: (b * nq + j, 0)),
        out_shape=jax.ShapeDtypeStruct((batch * tq_total, d_attn), BF16),
        scratch_shapes=[
            pltpu.VMEM((lp, tq), I32),
            pltpu.VMEM((lp, tq), F32),
            pltpu.VMEM((d_attn, tq), F32),
        ],
        compiler_params=pltpu.CompilerParams(
            dimension_semantics=("arbitrary", "arbitrary"), vmem_limit_bytes=VMEM_LIMIT),
        name="attn",
    )(qi3, wit, q, ki3, kb, vt)


def _rope_tables(pos):
    half = ROT_DIM // 2
    inv = ROPE_THETA ** (-np.arange(half, dtype=np.float64) * (2.0 / ROT_DIM))
    ang = np.asarray(pos, np.float64)[:, None] * inv[None, :]
    cos = np.ones((len(pos), HEAD_DIM))
    slo = np.zeros((len(pos), HEAD_DIM))
    shi = np.zeros((len(pos), HEAD_DIM))
    cos[:, :half] = np.cos(ang)
    cos[:, half:ROT_DIM] = np.cos(ang)
    slo[:, :half] = -np.sin(ang)
    shi[:, half:ROT_DIM] = np.sin(ang)
    rep = LANES // HEAD_DIM
    return tuple(jnp.asarray(np.tile(t, (1, rep)), F32) for t in (cos, slo, shi))


def _pad_rows(a, rows):
    return jnp.pad(a, ((0, 0), (0, rows - a.shape[1])) + ((0, 0),) * (a.ndim - 2))


def _layer(x, past, cache, wts, *, tm_ffn, tm_proj, tq, kc):
    (ffn1, projw, woc, woa, ffn2) = wts
    b, t, d = x.shape
    n = b * t
    cache_k, cache_v, cache_ik, conv_prev = cache
    x1 = _ffn_call(x.reshape(n, d), *ffn1, tm=tm_ffn)

    d_conv = conv_prev.shape[-1]
    cprev = jnp.pad(conv_prev, ((0, 0), (SUBLANES - (CONV_W - 1), 0), (0, 0)))
    tables = _rope_tables(past + np.arange(t))
    conv_out, q, k, v, kb, qi3, kiwi, ki3, ulast = _proj_call(
        x1, cprev, projw, tables, tm_proj, t // tm_proj)

    n_keys = past + t
    lp = -(-n_keys // kc) * kc
    kb_all = kb.reshape(b, t, LANES)
    v_all = v.reshape(b, t, LANES).astype(BF16)
    ki3_all = ki3.reshape(b, t, 2 * LANES)
    if past:
        ck = cache_k.reshape(b, past, LANES).astype(BF16)
        cv = cache_v.reshape(b, past, LANES).astype(BF16)
        cik = jnp.pad(cache_ik, ((0, 0), (0, 0), (0, LANES - IDX_DIM))).reshape(b * past, LANES)
        cik3 = _stack_keys_call(cik, past).reshape(b, past, 2 * LANES)
        kb_all = jnp.concatenate([ck, kb_all], axis=1)
        v_all = jnp.concatenate([cv, v_all], axis=1)
        ki3_all = jnp.concatenate([cik3, ki3_all], axis=1)
    kb_all = _pad_rows(kb_all, lp)
    ki3_all = _pad_rows(ki3_all, lp)
    vt = jnp.swapaxes(_pad_rows(v_all, lp).reshape(b, lp // kc, kc, LANES), 2, 3)

    tqp = -(-t // tq) * tq
    wi = kiwi[:, IDX_DIM:IDX_DIM + N_IDX_HEADS].reshape(b, t, N_IDX_HEADS)
    wit = jnp.swapaxes(_pad_rows(wi, tqp), 1, 2)
    padq = lambda a: _pad_rows(a.reshape(b, t, a.shape[-1]), tqp).reshape(b * tqp, a.shape[-1])
    n_sel = max(1, min(TOP_K, n_keys // 4))
    attn = _attn_call(padq(qi3), wit, padq(q), ki3_all, kb_all, vt,
                      batch=b, tq=tq, kc=kc, past=past, n_keys=n_keys, n_sel=n_sel)
    attn = attn.reshape(b, tqp, -1)[:, :t].reshape(n, -1)

    y = _out_ffn_call(x1, conv_out, attn, woc, woa, *ffn2, tm=tm_ffn)

    tiles = t // tm_proj
    conv_state = ulast.reshape(b, tiles, SUBLANES, d_conv)[:, -1, SUBLANES - (CONV_W - 1):]
    return (y.reshape(b, t, d),
            k.reshape(b, t, N_KV_HEADS, HEAD_DIM),
            v.reshape(b, t, N_KV_HEADS, HEAD_DIM),
            kiwi[:, :IDX_DIM].reshape(b, t, IDX_DIM),
            conv_state)


def _prep_weights(l, ffn1_norm, ffn1_w_gate, ffn1_w_up, ffn1_w_down, mix_norm, w_in, conv_w,
                  q_norm, k_norm, idx_k_norm, w_out, ffn2_norm, ffn2_w_gate, ffn2_w_up, ffn2_w_down):
    d_conv = conv_w.shape[-1]
    n_main = 3 * d_conv + N_HEADS * HEAD_DIM + 2 * N_KV_HEADS * HEAD_DIM
    row = lambda g: g[l][None, :]
    ffn1 = (row(ffn1_norm), ffn1_w_gate[l].astype(BF16), ffn1_w_up[l].astype(BF16), ffn1_w_down[l].astype(BF16))
    ffn2 = (row(ffn2_norm), ffn2_w_gate[l].astype(BF16), ffn2_w_up[l].astype(BF16), ffn2_w_down[l].astype(BF16))
    w = w_in[l]
    wmain = w[:, :n_main].astype(BF16)
    widx = w[:, n_main:]
    widx = jnp.pad(widx, ((0, 0), (0, -widx.shape[1] % LANES)))
    wih = widx.astype(BF16)
    wil = (widx - wih.astype(F32)).astype(BF16)
    gq = jnp.tile(q_norm[l], LANES // HEAD_DIM)[None, :]
    gk = jnp.tile(k_norm[l], LANES // HEAD_DIM)[None, :]
    gik = jnp.pad(idx_k_norm[l], (0, LANES - IDX_DIM))[None, :]
    lane = np.arange(LANES)
    bd = jnp.asarray(lane[:, None] // HEAD_DIM == lane[None, :] // HEAD_DIM, BF16)
    projw = (row(mix_norm), wmain, wih, wil, conv_w[l], gq, gk, gik, bd)
    wo = w_out[l].astype(BF16)
    return (ffn1, projw, wo[:d_conv], wo[d_conv:], ffn2)


def kernel(x_prompt, x_sample, cache_k, cache_v, cache_idx_k, state_conv, ffn1_norm, ffn1_w_gate, ffn1_w_up, ffn1_w_down, mix_norm, w_in, conv_w, q_norm, k_norm, idx_k_norm, w_out, ffn2_norm, ffn2_w_gate, ffn2_w_up, ffn2_w_down):
    depth = w_in.shape[0]
    bp, tp, _ = x_prompt.shape
    bs, ts, _ = x_sample.shape
    past = cache_k.shape[2]
    d_conv = conv_w.shape[-1]
    hp, hs = x_prompt, x_sample
    outs_p, outs_s = [], []
    for l in range(depth):
        wts = _prep_weights(l, ffn1_norm, ffn1_w_gate, ffn1_w_up, ffn1_w_down, mix_norm, w_in, conv_w,
                            q_norm, k_norm, idx_k_norm, w_out, ffn2_norm, ffn2_w_gate, ffn2_w_up, ffn2_w_down)
        empty = (None, None, None, jnp.zeros((bp, CONV_W - 1, d_conv), F32))
        hp, *rest_p = _layer(hp, 0, empty, wts, tm_ffn=256, tm_proj=256, tq=256, kc=256)
        cache = (cache_k[l], cache_v[l], cache_idx_k[l], state_conv[l])
        hs, *rest_s = _layer(hs, past, cache, wts, tm_ffn=bs * ts, tm_proj=ts, tq=128, kc=256)
        outs_p.append(rest_p)
        outs_s.append(rest_s)
    stack = lambda outs, i: jnp.stack([o[i] for o in outs])
    return (hp, hs,
            stack(outs_p, 0), stack(outs_p, 1), stack(outs_p, 2), stack(outs_p, 3),
            stack(outs_s, 0), stack(outs_s, 1), stack(outs_s, 2), stack(outs_s, 3))
```

```python
import functools

import numpy as np
import jax
import jax.numpy as jnp
from jax import lax
from jax.experimental import pallas as pl
from jax.experimental.pallas import tpu as pltpu

F32 = jnp.float32
BF16 = jnp.bfloat16
I32 = jnp.int32

CHUNK = 64
CONV_W = 3
N_HEADS = 8
N_KV_HEADS = 2
HEAD_DIM = 64
ROT_DIM = 16
ROPE_THETA = 500000.0
N_IDX_HEADS = 8
IDX_DIM = 64
TOP_K = 256
EPS = 1e-6

LANES = 128
SUBLANES = 8
INT_MIN = -(2 ** 31)
LOG2E = 1.4426950408889634
NEG_BIG = -1e30
VMEM_LIMIT = 56 * 1024 * 1024


def _const_spec(shape):
    nd = len(shape)
    return pl.BlockSpec(shape, lambda *_: (0,) * nd, pipeline_mode=pl.Buffered(1))


def _dot(a, b):
    return jnp.dot(a, b, preferred_element_type=F32)


def _dot_nt(a, b):
    return lax.dot_general(a, b, (((1,), (1,)), ((), ())), preferred_element_type=F32)


def _split_bf16(x):
    hi = x.astype(BF16).astype(F32)
    lo = (x - hi).astype(BF16).astype(F32)
    return hi, lo


def _rmsnorm(x, g):
    ms = jnp.mean(x * x, axis=-1, keepdims=True)
    return (x * lax.rsqrt(ms + EPS)) * g


def _swiglu(h, wg_ref, wu_ref, wd_ref):
    g = _dot(h, wg_ref[...])
    u = _dot(h, wu_ref[...])
    a = (g * jax.nn.sigmoid(g)) * u
    return _dot(a.astype(BF16), wd_ref[...])


def _ffn_kernel(x_ref, g_ref, wg_ref, wu_ref, wd_ref, o_ref):
    x = x_ref[...]
    h = _rmsnorm(x, g_ref[...]).astype(BF16)
    o_ref[...] = x + 0.5 * _swiglu(h, wg_ref, wu_ref, wd_ref)


def _ffn_call(x, g, wg, wu, wd, tm):
    n, d = x.shape
    dff = wg.shape[1]
    return pl.pallas_call(
        _ffn_kernel,
        grid=(n // tm,),
        in_specs=[
            pl.BlockSpec((tm, d), lambda i: (i, 0)),
            _const_spec((1, d)),
            _const_spec((d, dff)),
            _const_spec((d, dff)),
            _const_spec((dff, d)),
        ],
        out_specs=pl.BlockSpec((tm, d), lambda i: (i, 0)),
        out_shape=jax.ShapeDtypeStruct((n, d), F32),
        compiler_params=pltpu.CompilerParams(
            dimension_semantics=("arbitrary",), vmem_limit_bytes=VMEM_LIMIT),
        name="ffn",
    )(x, g, wg, wu, wd)


def _out_ffn_kernel(x_ref, c_ref, a_ref, woc_ref, woa_ref, g_ref, wg_ref, wu_ref, wd_ref, o_ref):
    x = x_ref[...] + (_dot(c_ref[...], woc_ref[...]) + _dot(a_ref[...], woa_ref[...]))
    h = _rmsnorm(x, g_ref[...]).astype(BF16)
    o_ref[...] = x + 0.5 * _swiglu(h, wg_ref, wu_ref, wd_ref)


def _out_ffn_call(x, conv_out, attn, woc, woa, g, wg, wu, wd, tm):
    n, d = x.shape
    dff = wg.shape[1]
    dc = conv_out.shape[1]
    da = attn.shape[1]
    return pl.pallas_call(
        _out_ffn_kernel,
        grid=(n // tm,),
        in_specs=[
            pl.BlockSpec((tm, d), lambda i: (i, 0)),
            pl.BlockSpec((tm, dc), lambda i: (i, 0)),
            pl.BlockSpec((tm, da), lambda i: (i, 0)),
            _const_spec((dc, d)),
            _const_spec((da, d)),
            _const_spec((1, d)),
            _const_spec((d, dff)),
            _const_spec((d, dff)),
            _const_spec((dff, d)),
        ],
        out_specs=pl.BlockSpec((tm, d), lambda i: (i, 0)),
        out_shape=jax.ShapeDtypeStruct((n, d), F32),
        compiler_params=pltpu.CompilerParams(
            dimension_semantics=("arbitrary",), vmem_limit_bytes=VMEM_LIMIT),
        name="out_ffn",
    )(x, conv_out, attn, woc, woa, g, wg, wu, wd)


def _rope(y, cos, sin_lo, sin_hi):
    half = ROT_DIM // 2
    return (y * cos + pltpu.roll(y, LANES - half, 1) * sin_lo) + pltpu.roll(y, half, 1) * sin_hi


def _group_sumsq(x, bd):
    hi, lo = _split_bf16(x * x)
    return _dot(hi.astype(BF16), bd) + _dot(lo.astype(BF16), bd)


def _head_norm(x, g, bd):
    ms = _group_sumsq(x, bd) * (1.0 / HEAD_DIM)
    return (x * lax.rsqrt(ms + EPS)) * g


def _stack_keys(kin):
    lane = lax.broadcasted_iota(I32, kin.shape, 1)
    first = lane < IDX_DIM
    hi, lo = _split_bf16(kin)
    c0 = jnp.where(first, hi, pltpu.roll(hi, IDX_DIM, 1))
    c1 = jnp.where(first, lo, 0.0)
    return jnp.concatenate([c0, c1], axis=1).astype(BF16)


def _stack_queries(qc):
    lane = lax.broadcasted_iota(I32, qc.shape, 1)
    first = lane < IDX_DIM
    hi, lo = _split_bf16(qc)
    hi_sw = pltpu.roll(hi, IDX_DIM, 1)
    lo_sw = pltpu.roll(lo, IDX_DIM, 1)
    a0 = jnp.where(first, hi, lo_sw)
    a1 = jnp.where(first, hi, 0.0)
    b0 = jnp.where(first, hi_sw, lo)
    b1 = jnp.where(first, hi_sw, 0.0)
    return jnp.concatenate([a0, a1, b0, b1], axis=1).astype(BF16)


def _proj_kernel(x_ref, gmix_ref, wmain_ref, wih_ref, wil_ref, convw_ref, cprev_ref,
                 gq_ref, gk_ref, gik_ref, cos_ref, slo_ref, shi_ref, bd_ref,
                 convout_ref, q_ref, k_ref, v_ref, kb_ref, qi3_ref, kiwi_ref, ki3_ref, ulast_ref,
                 ubuf_ref, *, tiles_per_batch, d_conv):
    i = pl.program_id(0)
    tm = x_ref.shape[0]
    h = _rmsnorm(x_ref[...], gmix_ref[...])
    h_hi = h.astype(BF16)
    h_lo = (h - h_hi.astype(F32)).astype(BF16)
    zm = _dot(h_hi, wmain_ref[...])
    wih = wih_ref[...]
    zi = (_dot(h_hi, wih) + _dot(h_lo, wih)) + _dot(h_hi, wil_ref[...])

    gb = zm[:, 0:d_conv]
    u = zm[:, d_conv:2 * d_conv] * zm[:, 2 * d_conv:3 * d_conv]

    @pl.when(i % tiles_per_batch == 0)
    def _():
        ubuf_ref[0:SUBLANES, :] = cprev_ref[0]

    ubuf_ref[SUBLANES:SUBLANES + tm, :] = u
    w = convw_ref[...]
    conv = (ubuf_ref[SUBLANES - 2:SUBLANES - 2 + tm, :] * w[0:1, :]
            + ubuf_ref[SUBLANES - 1:SUBLANES - 1 + tm, :] * w[1:2, :]) + u * w[2:3, :]
    convout_ref[...] = (gb * conv).astype(BF16)
    tail = u[tm - SUBLANES:tm, :]
    ulast_ref[0] = tail
    ubuf_ref[0:SUBLANES, :] = tail

    cos = cos_ref[...]
    slo = slo_ref[...]
    shi = shi_ref[...]
    bd = bd_ref[...]
    q0 = 3 * d_conv
    nq = N_HEADS * HEAD_DIM
    scale = HEAD_DIM ** -0.5 * LOG2E
    for c in range(nq // LANES):
        qc = zm[:, q0 + c * LANES:q0 + (c + 1) * LANES]
        qn = _head_norm(qc, gq_ref[...], bd)
        q_ref[:, c * LANES:(c + 1) * LANES] = (_rope(qn, cos, slo, shi) * scale).astype(BF16)
    k0 = q0 + nq
    kn = _rope(_head_norm(zm[:, k0:k0 + LANES], gk_ref[...], bd), cos, slo, shi)
    k_ref[...] = kn
    kb_ref[...] = kn.astype(BF16)
    v_ref[...] = zm[:, k0 + LANES:k0 + 2 * LANES]

    ni = N_IDX_HEADS * IDX_DIM
    for c in range(ni // LANES):
        qic = _rope(zi[:, c * LANES:(c + 1) * LANES], cos, slo, shi)
        qi3_ref[:, c * 4 * LANES:(c + 1) * 4 * LANES] = _stack_queries(qic)
    kw = zi[:, ni:ni + LANES]
    kin = _rope(_head_norm(kw, gik_ref[...], bd), cos, slo, shi)
    lane = lax.broadcasted_iota(I32, kw.shape, 1)
    kiwi_ref[...] = jnp.where(lane < IDX_DIM, kin, kw * (N_IDX_HEADS ** -0.5))
    ki3_ref[...] = _stack_keys(kin)


def _proj_call(x1, cprev, wts, tables, tm, tiles_per_batch):
    n, d = x1.shape
    (gmix, wmain, wih, wil, convw, gq, gk, gik, bd) = wts
    cos, slo, shi = tables
    d_conv = convw.shape[1]
    nt = n // tm
    row = lambda w: pl.BlockSpec((tm, w), lambda i: (i, 0))
    tab = pl.BlockSpec((tm, LANES), lambda i: (i % tiles_per_batch, 0))
    nq = N_HEADS * HEAD_DIM
    out_shape = (
        jax.ShapeDtypeStruct((n, d_conv), BF16),
        jax.ShapeDtypeStruct((n, nq), BF16),
        jax.ShapeDtypeStruct((n, LANES), F32),
        jax.ShapeDtypeStruct((n, LANES), F32),
        jax.ShapeDtypeStruct((n, LANES), BF16),
        jax.ShapeDtypeStruct((n, N_IDX_HEADS * 2 * LANES), BF16),
        jax.ShapeDtypeStruct((n, LANES), F32),
        jax.ShapeDtypeStruct((n, 2 * LANES), BF16),
        jax.ShapeDtypeStruct((nt, SUBLANES, d_conv), F32),
    )
    out_specs = (
        row(d_conv), row(nq), row(LANES), row(LANES), row(LANES), row(N_IDX_HEADS * 2 * LANES),
        row(LANES), row(2 * LANES),
        pl.BlockSpec((1, SUBLANES, d_conv), lambda i: (i, 0, 0)),
    )
    in_specs = [
        row(d),
        _const_spec(gmix.shape), _const_spec(wmain.shape), _const_spec(wih.shape), _const_spec(wil.shape),
        _const_spec(convw.shape),
        pl.BlockSpec((1, SUBLANES, d_conv), lambda i: (i // tiles_per_batch, 0, 0)),
        _const_spec(gq.shape), _const_spec(gk.shape), _const_spec(gik.shape),
        tab, tab, tab,
        _const_spec(bd.shape),
    ]
    return pl.pallas_call(
        functools.partial(_proj_kernel, tiles_per_batch=tiles_per_batch, d_conv=d_conv),
        grid=(nt,),
        in_specs=in_specs,
        out_specs=out_specs,
        out_shape=out_shape,
        scratch_shapes=[pltpu.VMEM((tm + SUBLANES, d_conv), F32)],
        compiler_params=pltpu.CompilerParams(
            dimension_semantics=("arbitrary",), vmem_limit_bytes=VMEM_LIMIT),
        name="proj",
    )(x1, gmix, wmain, wih, wil, convw, cprev, gq, gk, gik, cos, slo, shi, bd)


def _stack_keys_kernel(k_ref, o_ref):
    o_ref[...] = _stack_keys(k_ref[...])


def _stack_keys_call(kpad, tm):
    n = kpad.shape[0]
    return pl.pallas_call(
        _stack_keys_kernel,
        grid=(n // tm,),
        in_specs=[pl.BlockSpec((tm, LANES), lambda i: (i, 0))],
        out_specs=pl.BlockSpec((tm, 2 * LANES), lambda i: (i, 0)),
        out_shape=jax.ShapeDtypeStruct((n, 2 * LANES), BF16),
        name="stack_keys",
    )(kpad)


def _attn_kernel(qi3_ref, wit_ref, q_ref, ki3_ref, kb_ref, vt_ref, o_ref,
                 key_ref, bias_ref, ot_ref, m_ref, l_ref, s_ref, *, kc, past, n_keys, n_sel, idx_bits):
    tq = q_ref.shape[0]
    j = pl.program_id(1)
    q_first = past + j * tq
    pos = q_first + lax.broadcasted_iota(I32, (1, tq), 1)
    limit = jnp.minimum(((pos >> 6) + 1) * CHUNK, n_keys)
    max_limit = jnp.minimum((((q_first + tq - 1) >> 6) + 1) * CHUNK, n_keys)
    nk = (max_limit + kc - 1) // kc

    def rows(c):
        return pl.ds(pl.multiple_of(c * kc, kc), kc)

    def key_pos(c):
        return c * kc + lax.broadcasted_iota(I32, (kc, tq), 0)

    w8 = wit_ref[0] * (IDX_DIM ** -0.5)

    def score_body(c, carry):
        kk = ki3_ref[rows(c), :]
        acc = jnp.zeros((kc, tq), F32)
        for hh in range(N_IDX_HEADS):
            d = _dot_nt(kk, qi3_ref[:, hh * 2 * LANES:(hh + 1) * 2 * LANES])
            acc = acc + jnp.maximum(d, 0.0) * w8[hh:hh + 1, :]
        bits = pltpu.bitcast(acc + 0.0, I32)
        key = bits ^ ((bits >> 31) & 0x7FFFFFFF)
        key_ref[rows(c), :] = jnp.where(key_pos(c) < limit, key, INT_MIN)
        return carry

    lax.fori_loop(0, nk, score_body, 0)

    def count(pred):
        def body(c, acc):
            m = pred(c, key_ref[rows(c), :]).astype(I32)
            return acc + m.reshape(kc // SUBLANES, SUBLANES, tq).sum(axis=0)
        acc = lax.fori_loop(0, nk, body, jnp.zeros((SUBLANES, tq), I32))
        return acc.sum(axis=0, keepdims=True)

    c_nonneg = count(lambda c, k: k >= 0)
    thr0 = jnp.where(c_nonneg >= n_sel, 0, INT_MIN).astype(I32)

    def bit_body(b, thr):
        cand = thr | (jnp.int32(1) << (30 - b))
        cnt = count(lambda c, k: k >= cand)
        return jnp.where(cnt >= n_sel, cand, thr)

    thr = lax.fori_loop(0, 31, bit_body, thr0)

    c_gt = count(lambda c, k: k > thr)
    c_ge = count(lambda c, k: k >= thr)
    room = n_sel - c_gt
    live = thr != INT_MIN
    surplus = jnp.max(jnp.where(live & (c_ge > n_sel), 1, 0))

    def tie_search():
        def body(b, jm):
            cand = jm | (jnp.int32(1) << (idx_bits - 1 - b))
            cnt = count(lambda c, k: (k == thr) & (key_pos(c) < cand))
            return jnp.where(cnt < room, cand, jm)
        return lax.fori_loop(0, idx_bits, body, jnp.zeros((1, tq), I32))

    jm = lax.cond(surplus > 0, tie_search, lambda: jnp.full((1, tq), 2 ** idx_bits, I32))
    jm = jnp.where(live, jm, -1)

    def bias_body(c, carry):
        k = key_ref[rows(c), :]
        sel = (k > thr) | ((k == thr) & (key_pos(c) <= jm))
        bias_ref[rows(c), :] = jnp.where(sel, 0.0, NEG_BIG)
        return carry

    lax.fori_loop(0, nk, bias_body, 0)

    group = N_HEADS // N_KV_HEADS
    m_ref[...] = jnp.full(m_ref.shape, NEG_BIG, F32)
    l_ref[...] = jnp.zeros(l_ref.shape, F32)
    ot_ref[...] = jnp.zeros(ot_ref.shape, F32)

    def fold(x, op):
        return op(x.reshape(kc // SUBLANES, SUBLANES, tq), axis=0)

    def logit_body(c, carry):
        for hh in range(N_HEADS):
            kv = hh // group
            part = slice(hh * SUBLANES, (hh + 1) * SUBLANES)
            kk = kb_ref[rows(c), kv * HEAD_DIM:(kv + 1) * HEAD_DIM]
            s = _dot_nt(kk, q_ref[:, hh * HEAD_DIM:(hh + 1) * HEAD_DIM]) + bias_ref[rows(c), :]
            s_ref[hh, rows(c), :] = s
            m_ref[part, :] = jnp.maximum(m_ref[part, :], fold(s, jnp.max))
        return carry

    lax.fori_loop(0, nk, logit_body, 0)

    def pv_body(c, carry):
        for hh in range(N_HEADS):
            kv = hh // group
            part = slice(hh * SUBLANES, (hh + 1) * SUBLANES)
            head = slice(hh * HEAD_DIM, (hh + 1) * HEAD_DIM)
            m = m_ref[part, :].max(axis=0, keepdims=True)
            p = jnp.exp2(s_ref[hh, rows(c), :] - m)
            l_ref[part, :] = l_ref[part, :] + fold(p, jnp.sum)
            vt = vt_ref[c, kv * HEAD_DIM:(kv + 1) * HEAD_DIM, :]
            ot_ref[head, :] = ot_ref[head, :] + _dot(vt, p.astype(BF16))
        return carry

    lax.fori_loop(0, nk, pv_body, 0)
    for hh in range(N_HEADS):
        part = slice(hh * SUBLANES, (hh + 1) * SUBLANES)
        head = slice(hh * HEAD_DIM, (hh + 1) * HEAD_DIM)
        ot_ref[head, :] = ot_ref[head, :] / l_ref[part, :].sum(axis=0, keepdims=True)
    o_ref[...] = ot_ref[...].T.astype(BF16)


def _attn_call(qi3, wit, q, ki3, kb, vt, *, batch, tq, kc, past, n_keys, n_sel):
    tq_total = q.shape[0] // batch
    nq = tq_total // tq
    lp = ki3.shape[1]
    idx_bits = max(1, int(np.ceil(np.log2(lp))))
    d_attn = q.shape[1]
    kernel = functools.partial(_attn_kernel, kc=kc, past=past, n_keys=n_keys, n_sel=n_sel, idx_bits=idx_bits)
    return pl.pallas_call(
        kernel,
        grid=(batch, nq),
        in_specs=[
            pl.BlockSpec((tq, qi3.shape[1]), lambda b, j: (b * nq + j, 0)),
            pl.BlockSpec((1, N_IDX_HEADS, tq), lambda b, j: (b, 0, j)),
            pl.BlockSpec((tq, d_attn), lambda b, j: (b * nq + j, 0)),
            pl.BlockSpec((None, lp, ki3.shape[2]), lambda b, j: (b, 0, 0)),
            pl.BlockSpec((None, lp, kb.shape[2]), lambda b, j: (b, 0, 0)),
            pl.BlockSpec((None, lp // kc, vt.shape[2], kc), lambda b, j: (b, 0, 0, 0)),
        ],
        out_specs=pl.BlockSpec((tq, d_attn), lambda b, j: (b * nq + j, 0)),
        out_shape=jax.ShapeDtypeStruct((batch * tq_total, d_attn), BF16),
        scratch_shapes=[
            pltpu.VMEM((lp, tq), I32),
            pltpu.VMEM((lp, tq), F32),
            pltpu.VMEM((d_attn, tq), F32),
            pltpu.VMEM((N_HEADS * SUBLANES, tq), F32),
            pltpu.VMEM((N_HEADS * SUBLANES, tq), F32),
            pltpu.VMEM((N_HEADS, lp, tq), F32),
        ],
        compiler_params=pltpu.CompilerParams(
            dimension_semantics=("arbitrary", "arbitrary"), vmem_limit_bytes=VMEM_LIMIT),
        name="attn",
    )(qi3, wit, q, ki3, kb, vt)


def _rope_tables(pos):
    half = ROT_DIM // 2
    inv = ROPE_THETA ** (-np.arange(half, dtype=np.float64) * (2.0 / ROT_DIM))
    ang = np.asarray(pos, np.float64)[:, None] * inv[None, :]
    cos = np.ones((len(pos), HEAD_DIM))
    slo = np.zeros((len(pos), HEAD_DIM))
    shi = np.zeros((len(pos), HEAD_DIM))
    cos[:, :half] = np.cos(ang)
    cos[:, half:ROT_DIM] = np.cos(ang)
    slo[:, :half] = -np.sin(ang)
    shi[:, half:ROT_DIM] = np.sin(ang)
    rep = LANES // HEAD_DIM
    return tuple(jnp.asarray(np.tile(t, (1, rep)), F32) for t in (cos, slo, shi))


def _pad_rows(a, rows):
    return jnp.pad(a, ((0, 0), (0, rows - a.shape[1])) + ((0, 0),) * (a.ndim - 2))


def _layer(x, past, cache, wts, *, tm_ffn, tm_proj, tq, kc):
    (ffn1, projw, woc, woa, ffn2) = wts
    b, t, d = x.shape
    n = b * t
    cache_k, cache_v, cache_ik, conv_prev = cache
    x1 = _ffn_call(x.reshape(n, d), *ffn1, tm=tm_ffn)

    d_conv = conv_prev.shape[-1]
    cprev = jnp.pad(conv_prev, ((0, 0), (SUBLANES - (CONV_W - 1), 0), (0, 0)))
    tables = _rope_tables(past + np.arange(t))
    conv_out, q, k, v, kb, qi3, kiwi, ki3, ulast = _proj_call(
        x1, cprev, projw, tables, tm_proj, t // tm_proj)

    n_keys = past + t
    lp = -(-n_keys // kc) * kc
    kb_all = kb.reshape(b, t, LANES)
    v_all = v.reshape(b, t, LANES).astype(BF16)
    ki3_all = ki3.reshape(b, t, 2 * LANES)
    if past:
        ck = cache_k.reshape(b, past, LANES).astype(BF16)
        cv = cache_v.reshape(b, past, LANES).astype(BF16)
        cik = jnp.pad(cache_ik, ((0, 0), (0, 0), (0, LANES - IDX_DIM))).reshape(b * past, LANES)
        cik3 = _stack_keys_call(cik, past).reshape(b, past, 2 * LANES)
        kb_all = jnp.concatenate([ck, kb_all], axis=1)
        v_all = jnp.concatenate([cv, v_all], axis=1)
        ki3_all = jnp.concatenate([cik3, ki3_all], axis=1)
    kb_all = _pad_rows(kb_all, lp)
    ki3_all = _pad_rows(ki3_all, lp)
    vt = jnp.swapaxes(_pad_rows(v_all, lp).reshape(b, lp // kc, kc, LANES), 2, 3)

    tqp = -(-t // tq) * tq
    wi = kiwi[:, IDX_DIM:IDX_DIM + N_IDX_HEADS].reshape(b, t, N_IDX_HEADS)
    wit = jnp.swapaxes(_pad_rows(wi, tqp), 1, 2)
    padq = lambda a: _pad_rows(a.reshape(b, t, a.shape[-1]), tqp).reshape(b * tqp, a.shape[-1])
    n_sel = max(1, min(TOP_K, n_keys // 4))
    attn = _attn_call(padq(qi3), wit, padq(q), ki3_all, kb_all, vt,
                      batch=b, tq=tq, kc=kc, past=past, n_keys=n_keys, n_sel=n_sel)
    attn = attn.reshape(b, tqp, -1)[:, :t].reshape(n, -1)

    y = _out_ffn_call(x1, conv_out, attn, woc, woa, *ffn2, tm=tm_ffn)

    tiles = t // tm_proj
    conv_state = ulast.reshape(b, tiles, SUBLANES, d_conv)[:, -1, SUBLANES - (CONV_W - 1):]
    return (y.reshape(b, t, d),
            k.reshape(b, t, N_KV_HEADS, HEAD_DIM),
            v.reshape(b, t, N_KV_HEADS, HEAD_DIM),
            kiwi[:, :IDX_DIM].reshape(b, t, IDX_DIM),
            conv_state)


def _prep_weights(l, ffn1_norm, ffn1_w_gate, ffn1_w_up, ffn1_w_down, mix_norm, w_in, conv_w,
                  q_norm, k_norm, idx_k_norm, w_out, ffn2_norm, ffn2_w_gate, ffn2_w_up, ffn2_w_down):
    d_conv = conv_w.shape[-1]
    n_main = 3 * d_conv + N_HEADS * HEAD_DIM + 2 * N_KV_HEADS * HEAD_DIM
    row = lambda g: g[l][None, :]
    ffn1 = (row(ffn1_norm), ffn1_w_gate[l].astype(BF16), ffn1_w_up[l].astype(BF16), ffn1_w_down[l].astype(BF16))
    ffn2 = (row(ffn2_norm), ffn2_w_gate[l].astype(BF16), ffn2_w_up[l].astype(BF16), ffn2_w_down[l].astype(BF16))
    w = w_in[l]
    wmain = w[:, :n_main].astype(BF16)
    widx = w[:, n_main:]
    widx = jnp.pad(widx, ((0, 0), (0, -widx.shape[1] % LANES)))
    wih = widx.astype(BF16)
    wil = (widx - wih.astype(F32)).astype(BF16)
    gq = jnp.tile(q_norm[l], LANES // HEAD_DIM)[None, :]
    gk = jnp.tile(k_norm[l], LANES // HEAD_DIM)[None, :]
    gik = jnp.pad(idx_k_norm[l], (0, LANES - IDX_DIM))[None, :]
    lane = np.arange(LANES)
    bd = jnp.asarray(lane[:, None] // HEAD_DIM == lane[None, :] // HEAD_DIM, BF16)
    projw = (row(mix_norm), wmain, wih, wil, conv_w[l], gq, gk, gik, bd)
    wo = w_out[l].astype(BF16)
    return (ffn1, projw, wo[:d_conv], wo[d_conv:], ffn2)


def kernel(x_prompt, x_sample, cache_k, cache_v, cache_idx_k, state_conv, ffn1_norm, ffn1_w_gate, ffn1_w_up, ffn1_w_down, mix_norm, w_in, conv_w, q_norm, k_norm, idx_k_norm, w_out, ffn2_norm, ffn2_w_gate, ffn2_w_up, ffn2_w_down):
    depth = w_in.shape[0]
    bp, tp, _ = x_prompt.shape
    bs, ts, _ = x_sample.shape
    past = cache_k.shape[2]
    d_conv = conv_w.shape[-1]
    hp, hs = x_prompt, x_sample
    outs_p, outs_s = [], []
    for l in range(depth):
        wts = _prep_weights(l, ffn1_norm, ffn1_w_gate, ffn1_w_up, ffn1_w_down, mix_norm, w_in, conv_w,
                            q_norm, k_norm, idx_k_norm, w_out, ffn2_norm, ffn2_w_gate, ffn2_w_up, ffn2_w_down)
        empty = (None, None, None, jnp.zeros((bp, CONV_W - 1, d_conv), F32))
        hp, *rest_p = _layer(hp, 0, empty, wts, tm_ffn=256, tm_proj=256, tq=256, kc=256)
        cache = (cache_k[l], cache_v[l], cache_idx_k[l], state_conv[l])
        hs, *rest_s = _layer(hs, past, cache, wts, tm_ffn=bs * ts, tm_proj=ts, tq=128, kc=256)
        outs_p.append(rest_p)
        outs_s.append(rest_s)
    stack = lambda outs, i: jnp.stack([o[i] for o in outs])
    return (hp, hs,
            stack(outs_p, 0), stack(outs_p, 1), stack(outs_p, 2), stack(outs_p, 3),
            stack(outs_s, 0), stack(outs_s, 1), stack(outs_s, 2), stack(outs_s, 3))
```

```python
import functools

import numpy as np
import jax
import jax.numpy as jnp
from jax import lax
from jax.experimental import pallas as pl
from jax.experimental.pallas import tpu as pltpu

F32 = jnp.float32
BF16 = jnp.bfloat16
I32 = jnp.int32

CHUNK = 64
CONV_W = 3
N_HEADS = 8
N_KV_HEADS = 2
HEAD_DIM = 64
ROT_DIM = 16
ROPE_THETA = 500000.0
N_IDX_HEADS = 8
IDX_DIM = 64
TOP_K = 256
EPS = 1e-6

LANES = 128
SUBLANES = 8
INT_MIN = -(2 ** 31)
LOG2E = 1.4426950408889634
NEG_BIG = -1e30
VMEM_LIMIT = 56 * 1024 * 1024


def _const_spec(shape):
    nd = len(shape)
    return pl.BlockSpec(shape, lambda *_: (0,) * nd, pipeline_mode=pl.Buffered(1))


def _dot(a, b):
    return jnp.dot(a, b, preferred_element_type=F32)


def _dot_nt(a, b):
    return lax.dot_general(a, b, (((1,), (1,)), ((), ())), preferred_element_type=F32)


def _split_bf16(x):
    hi = x.astype(BF16).astype(F32)
    lo = (x - hi).astype(BF16).astype(F32)
    return hi, lo


def _rmsnorm(x, g):
    ms = jnp.mean(x * x, axis=-1, keepdims=True)
    return (x * lax.rsqrt(ms + EPS)) * g


def _swiglu(h, wg_ref, wu_ref, wd_ref):
    g = _dot(h, wg_ref[...])
    u = _dot(h, wu_ref[...])
    a = (g * jax.nn.sigmoid(g)) * u
    return _dot(a.astype(BF16), wd_ref[...])


def _ffn_kernel(x_ref, g_ref, wg_ref, wu_ref, wd_ref, o_ref):
    x = x_ref[...]
    h = _rmsnorm(x, g_ref[...]).astype(BF16)
    o_ref[...] = x + 0.5 * _swiglu(h, wg_ref, wu_ref, wd_ref)


def _ffn_call(x, g, wg, wu, wd, tm):
    n, d = x.shape
    dff = wg.shape[1]
    return pl.pallas_call(
        _ffn_kernel,
        grid=(n // tm,),
        in_specs=[
            pl.BlockSpec((tm, d), lambda i: (i, 0)),
            _const_spec((1, d)),
            _const_spec((d, dff)),
            _const_spec((d, dff)),
            _const_spec((dff, d)),
        ],
        out_specs=pl.BlockSpec((tm, d), lambda i: (i, 0)),
        out_shape=jax.ShapeDtypeStruct((n, d), F32),
        compiler_params=pltpu.CompilerParams(
            dimension_semantics=("arbitrary",), vmem_limit_bytes=VMEM_LIMIT),
        name="ffn",
    )(x, g, wg, wu, wd)


def _out_ffn_kernel(x_ref, c_ref, a_ref, woc_ref, woa_ref, g_ref, wg_ref, wu_ref, wd_ref, o_ref):
    x = x_ref[...] + (_dot(c_ref[...], woc_ref[...]) + _dot(a_ref[...], woa_ref[...]))
    h = _rmsnorm(x, g_ref[...]).astype(BF16)
    o_ref[...] = x + 0.5 * _swiglu(h, wg_ref, wu_ref, wd_ref)


def _out_ffn_call(x, conv_out, attn, woc, woa, g, wg, wu, wd, tm):
    n, d = x.shape
    dff = wg.shape[1]
    dc = conv_out.shape[1]
    da = attn.shape[1]
    return pl.pallas_call(
        _out_ffn_kernel,
        grid=(n // tm,),
        in_specs=[
            pl.BlockSpec((tm, d), lambda i: (i, 0)),
            pl.BlockSpec((tm, dc), lambda i: (i, 0)),
            pl.BlockSpec((tm, da), lambda i: (i, 0)),
            _const_spec((dc, d)),
            _const_spec((da, d)),
            _const_spec((1, d)),
            _const_spec((d, dff)),
            _const_spec((d, dff)),
            _const_spec((dff, d)),
        ],
        out_specs=pl.BlockSpec((tm, d), lambda i: (i, 0)),
        out_shape=jax.ShapeDtypeStruct((n, d), F32),
        compiler_params=pltpu.CompilerParams(
            dimension_semantics=("arbitrary",), vmem_limit_bytes=VMEM_LIMIT),
        name="out_ffn",
    )(x, conv_out, attn, woc, woa, g, wg, wu, wd)


def _rope(y, cos, sin_lo, sin_hi):
    half = ROT_DIM // 2
    return (y * cos + pltpu.roll(y, LANES - half, 1) * sin_lo) + pltpu.roll(y, half, 1) * sin_hi


def _group_sumsq(x, bd):
    hi, lo = _split_bf16(x * x)
    return _dot(hi.astype(BF16), bd) + _dot(lo.astype(BF16), bd)


def _head_norm(x, g, bd):
    ms = _group_sumsq(x, bd) * (1.0 / HEAD_DIM)
    return (x * lax.rsqrt(ms + EPS)) * g


def _stack_keys(kin):
    lane = lax.broadcasted_iota(I32, kin.shape, 1)
    first = lane < IDX_DIM
    hi, lo = _split_bf16(kin)
    c0 = jnp.where(first, hi, pltpu.roll(hi, IDX_DIM, 1))
    c1 = jnp.where(first, lo, 0.0)
    return jnp.concatenate([c0, c1], axis=1).astype(BF16)


def _stack_queries(qc):
    lane = lax.broadcasted_iota(I32, qc.shape, 1)
    first = lane < IDX_DIM
    hi, lo = _split_bf16(qc)
    hi_sw = pltpu.roll(hi, IDX_DIM, 1)
    lo_sw = pltpu.roll(lo, IDX_DIM, 1)
    a0 = jnp.where(first, hi, lo_sw)
    a1 = jnp.where(first, hi, 0.0)
    b0 = jnp.where(first, hi_sw, lo)
    b1 = jnp.where(first, hi_sw, 0.0)
    return jnp.concatenate([a0, a1, b0, b1], axis=1).astype(BF16)


def _proj_kernel(x_ref, gmix_ref, wmain_ref, wih_ref, wil_ref, convw_ref, cprev_ref,
                 gq_ref, gk_ref, gik_ref, cos_ref, slo_ref, shi_ref, bd_ref,
                 convout_ref, q_ref, k_ref, v_ref, kb_ref, qi3_ref, kiwi_ref, ki3_ref, ulast_ref,
                 ubuf_ref, *, tiles_per_batch, d_conv):
    i = pl.program_id(0)
    tm = x_ref.shape[0]
    h = _rmsnorm(x_ref[...], gmix_ref[...])
    h_hi = h.astype(BF16)
    h_lo = (h - h_hi.astype(F32)).astype(BF16)
    zm = _dot(h_hi, wmain_ref[...])
    wih = wih_ref[...]
    zi = (_dot(h_hi, wih) + _dot(h_lo, wih)) + _dot(h_hi, wil_ref[...])

    gb = zm[:, 0:d_conv]
    u = zm[:, d_conv:2 * d_conv] * zm[:, 2 * d_conv:3 * d_conv]

    @pl.when(i % tiles_per_batch == 0)
    def _():
        ubuf_ref[0:SUBLANES, :] = cprev_ref[0]

    ubuf_ref[SUBLANES:SUBLANES + tm, :] = u
    w = convw_ref[...]
    conv = (ubuf_ref[SUBLANES - 2:SUBLANES - 2 + tm, :] * w[0:1, :]
            + ubuf_ref[SUBLANES - 1:SUBLANES - 1 + tm, :] * w[1:2, :]) + u * w[2:3, :]
    convout_ref[...] = (gb * conv).astype(BF16)
    tail = u[tm - SUBLANES:tm, :]
    ulast_ref[0] = tail
    ubuf_ref[0:SUBLANES, :] = tail

    cos = cos_ref[...]
    slo = slo_ref[...]
    shi = shi_ref[...]
    bd = bd_ref[...]
    q0 = 3 * d_conv
    nq = N_HEADS * HEAD_DIM
    scale = HEAD_DIM ** -0.5 * LOG2E
    for c in range(nq // LANES):
        qc = zm[:, q0 + c * LANES:q0 + (c + 1) * LANES]
        qn = _head_norm(qc, gq_ref[...], bd)
        q_ref[:, c * LANES:(c + 1) * LANES] = (_rope(qn, cos, slo, shi) * scale).astype(BF16)
    k0 = q0 + nq
    kn = _rope(_head_norm(zm[:, k0:k0 + LANES], gk_ref[...], bd), cos, slo, shi)
    k_ref[...] = kn
    kb_ref[...] = kn.astype(BF16)
    v_ref[...] = zm[:, k0 + LANES:k0 + 2 * LANES]

    ni = N_IDX_HEADS * IDX_DIM
    for c in range(ni // LANES):
        qic = _rope(zi[:, c * LANES:(c + 1) * LANES], cos, slo, shi)
        qi3_ref[:, c * 4 * LANES:(c + 1) * 4 * LANES] = _stack_queries(qic)
    kw = zi[:, ni:ni + LANES]
    kin = _rope(_head_norm(kw, gik_ref[...], bd), cos, slo, shi)
    lane = lax.broadcasted_iota(I32, kw.shape, 1)
    kiwi_ref[...] = jnp.where(lane < IDX_DIM, kin, kw * (N_IDX_HEADS ** -0.5))
    ki3_ref[...] = _stack_keys(kin)


def _proj_call(x1, cprev, wts, tables, tm, tiles_per_batch):
    n, d = x1.shape
    (gmix, wmain, wih, wil, convw, gq, gk, gik, bd) = wts
    cos, slo, shi = tables
    d_conv = convw.shape[1]
    nt = n // tm
    row = lambda w: pl.BlockSpec((tm, w), lambda i: (i, 0))
    tab = pl.BlockSpec((tm, LANES), lambda i: (i % tiles_per_batch, 0))
    nq = N_HEADS * HEAD_DIM
    out_shape = (
        jax.ShapeDtypeStruct((n, d_conv), BF16),
        jax.ShapeDtypeStruct((n, nq), BF16),
        jax.ShapeDtypeStruct((n, LANES), F32),
        jax.ShapeDtypeStruct((n, LANES), F32),
        jax.ShapeDtypeStruct((n, LANES), BF16),
        jax.ShapeDtypeStruct((n, N_IDX_HEADS * 2 * LANES), BF16),
        jax.ShapeDtypeStruct((n, LANES), F32),
        jax.ShapeDtypeStruct((n, 2 * LANES), BF16),
        jax.ShapeDtypeStruct((nt, SUBLANES, d_conv), F32),
    )
    out_specs = (
        row(d_conv), row(nq), row(LANES), row(LANES), row(LANES), row(N_IDX_HEADS * 2 * LANES),
        row(LANES), row(2 * LANES),
        pl.BlockSpec((1, SUBLANES, d_conv), lambda i: (i, 0, 0)),
    )
    in_specs = [
        row(d),
        _const_spec(gmix.shape), _const_spec(wmain.shape), _const_spec(wih.shape), _const_spec(wil.shape),
        _const_spec(convw.shape),
        pl.BlockSpec((1, SUBLANES, d_conv), lambda i: (i // tiles_per_batch, 0, 0)),
        _const_spec(gq.shape), _const_spec(gk.shape), _const_spec(gik.shape),
        tab, tab, tab,
        _const_spec(bd.shape),
    ]
    return pl.pallas_call(
        functools.partial(_proj_kernel, tiles_per_batch=tiles_per_batch, d_conv=d_conv),
        grid=(nt,),
        in_specs=in_specs,
        out_specs=out_specs,
        out_shape=out_shape,
        scratch_shapes=[pltpu.VMEM((tm + SUBLANES, d_conv), F32)],
        compiler_params=pltpu.CompilerParams(
            dimension_semantics=("arbitrary",), vmem_limit_bytes=VMEM_LIMIT),
        name="proj",
    )(x1, gmix, wmain, wih, wil, convw, cprev, gq, gk, gik, cos, slo, shi, bd)


def _stack_keys_kernel(k_ref, o_ref):
    o_ref[...] = _stack_keys(k_ref[...])


def _stack_keys_call(kpad, tm):
    n = kpad.shape[0]
    return pl.pallas_call(
        _stack_keys_kernel,
        grid=(n // tm,),
        in_specs=[pl.BlockSpec((tm, LANES), lambda i: (i, 0))],
        out_specs=pl.BlockSpec((tm, 2 * LANES), lambda i: (i, 0)),
        out_shape=jax.ShapeDtypeStruct((n, 2 * LANES), BF16),
        name="stack_keys",
    )(kpad)


def _attn_kernel(qi3_ref, wit_ref, q_ref, ki3_ref, kb_ref, vt_ref, o_ref,
                 score_ref, bias_ref, ot_ref, m_ref, l_ref, s_ref, *, kc, past, n_keys, n_sel, idx_bits):
    tq = q_ref.shape[0]
    j = pl.program_id(1)
    q_first = past + j * tq
    pos = q_first + lax.broadcasted_iota(I32, (1, tq), 1)
    limit = jnp.minimum(((pos >> 6) + 1) * CHUNK, n_keys)
    max_limit = jnp.minimum((((q_first + tq - 1) >> 6) + 1) * CHUNK, n_keys)
    nk = (max_limit + kc - 1) // kc

    def rows(c):
        return pl.ds(pl.multiple_of(c * kc, kc), kc)

    def key_pos(c):
        return c * kc + lax.broadcasted_iota(I32, (kc, tq), 0)

    w8 = wit_ref[0] * (IDX_DIM ** -0.5)

    def score_body(c, carry):
        kk = ki3_ref[rows(c), :]
        acc = jnp.zeros((kc, tq), F32)
        for hh in range(N_IDX_HEADS):
            d = _dot_nt(kk, qi3_ref[:, hh * 2 * LANES:(hh + 1) * 2 * LANES])
            acc = acc + jnp.maximum(d, 0.0) * w8[hh:hh + 1, :]
        score_ref[rows(c), :] = jnp.where(key_pos(c) < limit, acc + 0.0, -jnp.inf)
        return carry

    lax.fori_loop(0, nk, score_body, 0)

    def count(pred):
        def body(c, acc):
            m = pred(c, score_ref[rows(c), :]).astype(I32)
            return acc + m.reshape(kc // SUBLANES, SUBLANES, tq).sum(axis=0)
        acc = lax.fori_loop(0, nk, body, jnp.zeros((SUBLANES, tq), I32))
        return acc.sum(axis=0, keepdims=True)

    def as_float(t):
        return pltpu.bitcast(t ^ ((t >> 31) & 0x7FFFFFFF), F32)

    c_nonneg = count(lambda c, s: s >= 0.0)
    t0 = jnp.where(c_nonneg >= n_sel, 0, INT_MIN).astype(I32)

    def bit_body(b, t):
        cand = t | (jnp.int32(1) << (30 - b))
        cand_f = as_float(cand)
        cnt = count(lambda c, s: s >= cand_f)
        return jnp.where(cnt >= n_sel, cand, t)

    t_sel = lax.fori_loop(0, 31, bit_body, t0)
    live = t_sel != INT_MIN
    thr = jnp.where(live, as_float(t_sel), -jnp.inf)

    c_gt = count(lambda c, s: s > thr)
    c_ge = count(lambda c, s: s >= thr)
    room = n_sel - c_gt
    surplus = jnp.max(jnp.where(live & (c_ge > n_sel), 1, 0))

    def tie_search():
        def body(b, jm):
            cand = jm | (jnp.int32(1) << (idx_bits - 1 - b))
            cnt = count(lambda c, s: (s == thr) & (key_pos(c) < cand))
            return jnp.where(cnt < room, cand, jm)
        return lax.fori_loop(0, idx_bits, body, jnp.zeros((1, tq), I32))

    jm = lax.cond(surplus > 0, tie_search, lambda: jnp.full((1, tq), 2 ** idx_bits, I32))
    jm = jnp.where(live, jm, -1)

    def bias_body(c, carry):
        s = score_ref[rows(c), :]
        sel = (s > thr) | ((s == thr) & (key_pos(c) <= jm))
        bias_ref[rows(c), :] = jnp.where(sel, 0.0, NEG_BIG)
        return carry

    lax.fori_loop(0, nk, bias_body, 0)

    group = N_HEADS // N_KV_HEADS
    m_ref[...] = jnp.full(m_ref.shape, NEG_BIG, F32)
    l_ref[...] = jnp.zeros(l_ref.shape, F32)
    ot_ref[...] = jnp.zeros(ot_ref.shape, F32)

    def fold(x, op):
        return op(x.reshape(kc // SUBLANES, SUBLANES, tq), axis=0)

    def logit_body(c, carry):
        for hh in range(N_HEADS):
            kv = hh // group
            part = slice(hh * SUBLANES, (hh + 1) * SUBLANES)
            kk = kb_ref[rows(c), kv * HEAD_DIM:(kv + 1) * HEAD_DIM]
            s = _dot_nt(kk, q_ref[:, hh * HEAD_DIM:(hh + 1) * HEAD_DIM]) + bias_ref[rows(c), :]
            s_ref[hh, rows(c), :] = s
            m_ref[part, :] = jnp.maximum(m_ref[part, :], fold(s, jnp.max))
        return carry

    lax.fori_loop(0, nk, logit_body, 0)

    def pv_body(c, carry):
        for hh in range(N_HEADS):
            kv = hh // group
            part = slice(hh * SUBLANES, (hh + 1) * SUBLANES)
            head = slice(hh * HEAD_DIM, (hh + 1) * HEAD_DIM)
            m = m_ref[part, :].max(axis=0, keepdims=True)
            p = jnp.exp2(s_ref[hh, rows(c), :] - m)
            l_ref[part, :] = l_ref[part, :] + fold(p, jnp.sum)
            vt = vt_ref[c, kv * HEAD_DIM:(kv + 1) * HEAD_DIM, :]
            ot_ref[head, :] = ot_ref[head, :] + _dot(vt, p.astype(BF16))
        return carry

    lax.fori_loop(0, nk, pv_body, 0)
    for hh in range(N_HEADS):
        part = slice(hh * SUBLANES, (hh + 1) * SUBLANES)
        head = slice(hh * HEAD_DIM, (hh + 1) * HEAD_DIM)
        ot_ref[head, :] = ot_ref[head, :] / l_ref[part, :].sum(axis=0, keepdims=True)
    o_ref[...] = ot_ref[...].T.astype(BF16)


def _attn_call(qi3, wit, q, ki3, kb, vt, *, batch, tq, kc, past, n_keys, n_sel):
    tq_total = q.shape[0] // batch
    nq = tq_total // tq
    lp = ki3.shape[1]
    idx_bits = max(1, int(np.ceil(np.log2(lp))))
    d_attn = q.shape[1]
    kernel = functools.partial(_attn_kernel, kc=kc, past=past, n_keys=n_keys, n_sel=n_sel, idx_bits=idx_bits)
    return pl.pallas_call(
        kernel,
        grid=(batch, nq),
        in_specs=[
            pl.BlockSpec((tq, qi3.shape[1]), lambda b, j: (b * nq + j, 0)),
            pl.BlockSpec((1, N_IDX_HEADS, tq), lambda b, j: (b, 0, j)),
            pl.BlockSpec((tq, d_attn), lambda b, j: (b * nq + j, 0)),
            pl.BlockSpec((None, lp, ki3.shape[2]), lambda b, j: (b, 0, 0)),
            pl.BlockSpec((None, lp, kb.shape[2]), lambda b, j: (b, 0, 0)),
            pl.BlockSpec((None, lp // kc, vt.shape[2], kc), lambda b, j: (b, 0, 0, 0)),
        ],
        out_specs=pl.BlockSpec((tq, d_attn), lambda b, j: (b * nq + j, 0)),
        out_shape=jax.ShapeDtypeStruct((batch * tq_total, d_attn), BF16),
        scratch_shapes=[
            pltpu.VMEM((lp, tq), F32),
            pltpu.VMEM((lp, tq), F32),
            pltpu.VMEM((d_attn, tq), F32),
            pltpu.VMEM((N_HEADS * SUBLANES, tq), F32),
            pltpu.VMEM((N_HEADS * SUBLANES, tq), F32),
            pltpu.VMEM((N_HEADS, lp, tq), F32),
        ],
        compiler_params=pltpu.CompilerParams(
            dimension_semantics=("arbitrary", "arbitrary"), vmem_limit_bytes=VMEM_LIMIT),
        name="attn",
    )(qi3, wit, q, ki3, kb, vt)


def _rope_tables(pos):
    half = ROT_DIM // 2
    inv = ROPE_THETA ** (-np.arange(half, dtype=np.float64) * (2.0 / ROT_DIM))
    ang = np.asarray(pos, np.float64)[:, None] * inv[None, :]
    cos = np.ones((len(pos), HEAD_DIM))
    slo = np.zeros((len(pos), HEAD_DIM))
    shi = np.zeros((len(pos), HEAD_DIM))
    cos[:, :half] = np.cos(ang)
    cos[:, half:ROT_DIM] = np.cos(ang)
    slo[:, :half] = -np.sin(ang)
    shi[:, half:ROT_DIM] = np.sin(ang)
    rep = LANES // HEAD_DIM
    return tuple(jnp.asarray(np.tile(t, (1, rep)), F32) for t in (cos, slo, shi))


def _pad_rows(a, rows):
    return jnp.pad(a, ((0, 0), (0, rows - a.shape[1])) + ((0, 0),) * (a.ndim - 2))


def _layer(x, past, cache, wts, *, tm_ffn, tm_proj, tq, kc):
    (ffn1, projw, woc, woa, ffn2) = wts
    b, t, d = x.shape
    n = b * t
    cache_k, cache_v, cache_ik, conv_prev = cache
    x1 = _ffn_call(x.reshape(n, d), *ffn1, tm=tm_ffn)

    d_conv = conv_prev.shape[-1]
    cprev = jnp.pad(conv_prev, ((0, 0), (SUBLANES - (CONV_W - 1), 0), (0, 0)))
    tables = _rope_tables(past + np.arange(t))
    conv_out, q, k, v, kb, qi3, kiwi, ki3, ulast = _proj_call(
        x1, cprev, projw, tables, tm_proj, t // tm_proj)

    n_keys = past + t
    lp = -(-n_keys // kc) * kc
    kb_all = kb.reshape(b, t, LANES)
    v_all = v.reshape(b, t, LANES).astype(BF16)
    ki3_all = ki3.reshape(b, t, 2 * LANES)
    if past:
        ck = cache_k.reshape(b, past, LANES).astype(BF16)
        cv = cache_v.reshape(b, past, LANES).astype(BF16)
        cik = jnp.pad(cache_ik, ((0, 0), (0, 0), (0, LANES - IDX_DIM))).reshape(b * past, LANES)
        cik3 = _stack_keys_call(cik, past).reshape(b, past, 2 * LANES)
        kb_all = jnp.concatenate([ck, kb_all], axis=1)
        v_all = jnp.concatenate([cv, v_all], axis=1)
        ki3_all = jnp.concatenate([cik3, ki3_all], axis=1)
    kb_all = _pad_rows(kb_all, lp)
    ki3_all = _pad_rows(ki3_all, lp)
    vt = jnp.swapaxes(_pad_rows(v_all, lp).reshape(b, lp // kc, kc, LANES), 2, 3)

    tqp = -(-t // tq) * tq
    wi = kiwi[:, IDX_DIM:IDX_DIM + N_IDX_HEADS].reshape(b, t, N_IDX_HEADS)
    wit = jnp.swapaxes(_pad_rows(wi, tqp), 1, 2)
    padq = lambda a: _pad_rows(a.reshape(b, t, a.shape[-1]), tqp).reshape(b * tqp, a.shape[-1])
    n_sel = max(1, min(TOP_K, n_keys // 4))
    attn = _attn_call(padq(qi3), wit, padq(q), ki3_all, kb_all, vt,
                      batch=b, tq=tq, kc=kc, past=past, n_keys=n_keys, n_sel=n_sel)
    attn = attn.reshape(b, tqp, -1)[:, :t].reshape(n, -1)

    y = _out_ffn_call(x1, conv_out, attn, woc, woa, *ffn2, tm=tm_ffn)

    tiles = t // tm_proj
    conv_state = ulast.reshape(b, tiles, SUBLANES, d_conv)[:, -1, SUBLANES - (CONV_W - 1):]
    return (y.reshape(b, t, d),
            k.reshape(b, t, N_KV_HEADS, HEAD_DIM),
            v.reshape(b, t, N_KV_HEADS, HEAD_DIM),
            kiwi[:, :IDX_DIM].reshape(b, t, IDX_DIM),
            conv_state)


def _prep_weights(l, ffn1_norm, ffn1_w_gate, ffn1_w_up, ffn1_w_down, mix_norm, w_in, conv_w,
                  q_norm, k_norm, idx_k_norm, w_out, ffn2_norm, ffn2_w_gate, ffn2_w_up, ffn2_w_down):
    d_conv = conv_w.shape[-1]
    n_main = 3 * d_conv + N_HEADS * HEAD_DIM + 2 * N_KV_HEADS * HEAD_DIM
    row = lambda g: g[l][None, :]
    ffn1 = (row(ffn1_norm), ffn1_w_gate[l].astype(BF16), ffn1_w_up[l].astype(BF16), ffn1_w_down[l].astype(BF16))
    ffn2 = (row(ffn2_norm), ffn2_w_gate[l].astype(BF16), ffn2_w_up[l].astype(BF16), ffn2_w_down[l].astype(BF16))
    w = w_in[l]
    wmain = w[:, :n_main].astype(BF16)
    widx = w[:, n_main:]
    widx = jnp.pad(widx, ((0, 0), (0, -widx.shape[1] % LANES)))
    wih = widx.astype(BF16)
    wil = (widx - wih.astype(F32)).astype(BF16)
    gq = jnp.tile(q_norm[l], LANES // HEAD_DIM)[None, :]
    gk = jnp.tile(k_norm[l], LANES // HEAD_DIM)[None, :]
    gik = jnp.pad(idx_k_norm[l], (0, LANES - IDX_DIM))[None, :]
    lane = np.arange(LANES)
    bd = jnp.asarray(lane[:, None] // HEAD_DIM == lane[None, :] // HEAD_DIM, BF16)
    projw = (row(mix_norm), wmain, wih, wil, conv_w[l], gq, gk, gik, bd)
    wo = w_out[l].astype(BF16)
    return (ffn1, projw, wo[:d_conv], wo[d_conv:], ffn2)


def kernel(x_prompt, x_sample, cache_k, cache_v, cache_idx_k, state_conv, ffn1_norm, ffn1_w_gate, ffn1_w_up, ffn1_w_down, mix_norm, w_in, conv_w, q_norm, k_norm, idx_k_norm, w_out, ffn2_norm, ffn2_w_gate, ffn2_w_up, ffn2_w_down):
    depth = w_in.shape[0]
    bp, tp, _ = x_prompt.shape
    bs, ts, _ = x_sample.shape
    past = cache_k.shape[2]
    d_conv = conv_w.shape[-1]
    hp, hs = x_prompt, x_sample
    outs_p, outs_s = [], []
    for l in range(depth):
        wts = _prep_weights(l, ffn1_norm, ffn1_w_gate, ffn1_w_up, ffn1_w_down, mix_norm, w_in, conv_w,
                            q_norm, k_norm, idx_k_norm, w_out, ffn2_norm, ffn2_w_gate, ffn2_w_up, ffn2_w_down)
        empty = (None, None, None, jnp.zeros((bp, CONV_W - 1, d_conv), F32))
        hp, *rest_p = _layer(hp, 0, empty, wts, tm_ffn=256, tm_proj=256, tq=256, kc=256)
        cache = (cache_k[l], cache_v[l], cache_idx_k[l], state_conv[l])
        hs, *rest_s = _layer(hs, past, cache, wts, tm_ffn=bs * ts, tm_proj=ts, tq=128, kc=256)
        outs_p.append(rest_p)
        outs_s.append(rest_s)
    stack = lambda outs, i: jnp.stack([o[i] for o in outs])
    return (hp, hs,
            stack(outs_p, 0), stack(outs_p, 1), stack(outs_p, 2), stack(outs_p, 3),
            stack(outs_s, 0), stack(outs_s, 1), stack(outs_s, 2), stack(outs_s, 3))
```

```python
import functools

import numpy as np
import jax
import jax.numpy as jnp
from jax import lax
from jax.experimental import pallas as pl
from jax.experimental.pallas import tpu as pltpu

F32 = jnp.float32
BF16 = jnp.bfloat16
I32 = jnp.int32

CHUNK = 64
CONV_W = 3
N_HEADS = 8
N_KV_HEADS = 2
HEAD_DIM = 64
ROT_DIM = 16
ROPE_THETA = 500000.0
N_IDX_HEADS = 8
IDX_DIM = 64
TOP_K = 256
EPS = 1e-6

LANES = 128
SUBLANES = 8
INT_MIN = -(2 ** 31)
LOG2E = 1.4426950408889634
NEG_BIG = -1e30
VMEM_LIMIT = 56 * 1024 * 1024


def _const_spec(shape):
    nd = len(shape)
    return pl.BlockSpec(shape, lambda *_: (0,) * nd, pipeline_mode=pl.Buffered(1))


def _dot(a, b):
    return jnp.dot(a, b, preferred_element_type=F32)


def _dot_nt(a, b):
    return lax.dot_general(a, b, (((1,), (1,)), ((), ())), preferred_element_type=F32)


def _split_bf16(x):
    hi = x.astype(BF16).astype(F32)
    lo = (x - hi).astype(BF16).astype(F32)
    return hi, lo


def _rmsnorm(x, g):
    ms = jnp.mean(x * x, axis=-1, keepdims=True)
    return (x * lax.rsqrt(ms + EPS)) * g


def _swiglu(h, wg_ref, wu_ref, wd_ref):
    g = _dot(h, wg_ref[...])
    u = _dot(h, wu_ref[...])
    a = (g * jax.nn.sigmoid(g)) * u
    return _dot(a.astype(BF16), wd_ref[...])


def _ffn_kernel(x_ref, g_ref, wg_ref, wu_ref, wd_ref, o_ref):
    x = x_ref[...]
    h = _rmsnorm(x, g_ref[...]).astype(BF16)
    o_ref[...] = x + 0.5 * _swiglu(h, wg_ref, wu_ref, wd_ref)


def _ffn_call(x, g, wg, wu, wd, tm):
    n, d = x.shape
    dff = wg.shape[1]
    return pl.pallas_call(
        _ffn_kernel,
        grid=(n // tm,),
        in_specs=[
            pl.BlockSpec((tm, d), lambda i: (i, 0)),
            _const_spec((1, d)),
            _const_spec((d, dff)),
            _const_spec((d, dff)),
            _const_spec((dff, d)),
        ],
        out_specs=pl.BlockSpec((tm, d), lambda i: (i, 0)),
        out_shape=jax.ShapeDtypeStruct((n, d), F32),
        compiler_params=pltpu.CompilerParams(
            dimension_semantics=("arbitrary",), vmem_limit_bytes=VMEM_LIMIT),
        name="ffn",
    )(x, g, wg, wu, wd)


def _out_ffn_kernel(x_ref, c_ref, a_ref, woc_ref, woa_ref, g_ref, wg_ref, wu_ref, wd_ref, o_ref):
    x = x_ref[...] + (_dot(c_ref[...], woc_ref[...]) + _dot(a_ref[...], woa_ref[...]))
    h = _rmsnorm(x, g_ref[...]).astype(BF16)
    o_ref[...] = x + 0.5 * _swiglu(h, wg_ref, wu_ref, wd_ref)


def _out_ffn_call(x, conv_out, attn, woc, woa, g, wg, wu, wd, tm):
    n, d = x.shape
    dff = wg.shape[1]
    dc = conv_out.shape[1]
    da = attn.shape[1]
    return pl.pallas_call(
        _out_ffn_kernel,
        grid=(n // tm,),
        in_specs=[
            pl.BlockSpec((tm, d), lambda i: (i, 0)),
            pl.BlockSpec((tm, dc), lambda i: (i, 0)),
            pl.BlockSpec((tm, da), lambda i: (i, 0)),
            _const_spec((dc, d)),
            _const_spec((da, d)),
            _const_spec((1, d)),
            _const_spec((d, dff)),
            _const_spec((d, dff)),
            _const_spec((dff, d)),
        ],
        out_specs=pl.BlockSpec((tm, d), lambda i: (i, 0)),
        out_shape=jax.ShapeDtypeStruct((n, d), F32),
        compiler_params=pltpu.CompilerParams(
            dimension_semantics=("arbitrary",), vmem_limit_bytes=VMEM_LIMIT),
        name="out_ffn",
    )(x, conv_out, attn, woc, woa, g, wg, wu, wd)


def _rope(y, cos, sin_lo, sin_hi):
    half = ROT_DIM // 2
    return (y * cos + pltpu.roll(y, LANES - half, 1) * sin_lo) + pltpu.roll(y, half, 1) * sin_hi


def _group_sumsq(x, bd):
    hi, lo = _split_bf16(x * x)
    return _dot(hi.astype(BF16), bd) + _dot(lo.astype(BF16), bd)


def _head_norm(x, g, bd):
    ms = _group_sumsq(x, bd) * (1.0 / HEAD_DIM)
    return (x * lax.rsqrt(ms + EPS)) * g


def _stack_keys(kin):
    lane = lax.broadcasted_iota(I32, kin.shape, 1)
    first = lane < IDX_DIM
    hi, lo = _split_bf16(kin)
    c0 = jnp.where(first, hi, pltpu.roll(hi, IDX_DIM, 1))
    c1 = jnp.where(first, lo, 0.0)
    return jnp.concatenate([c0, c1], axis=1).astype(BF16)


def _stack_queries(qc):
    lane = lax.broadcasted_iota(I32, qc.shape, 1)
    first = lane < IDX_DIM
    hi, lo = _split_bf16(qc)
    hi_sw = pltpu.roll(hi, IDX_DIM, 1)
    lo_sw = pltpu.roll(lo, IDX_DIM, 1)
    a0 = jnp.where(first, hi, lo_sw)
    a1 = jnp.where(first, hi, 0.0)
    b0 = jnp.where(first, hi_sw, lo)
    b1 = jnp.where(first, hi_sw, 0.0)
    return jnp.concatenate([a0, a1, b0, b1], axis=1).astype(BF16)


def _proj_kernel(x_ref, gmix_ref, wmain_ref, wih_ref, wil_ref, convw_ref, cprev_ref,
                 gq_ref, gk_ref, gik_ref, cos_ref, slo_ref, shi_ref, bd_ref,
                 convout_ref, q_ref, k_ref, v_ref, kb_ref, qi3_ref, kiwi_ref, ki3_ref, ulast_ref,
                 ubuf_ref, *, tiles_per_batch, d_conv):
    i = pl.program_id(0)
    tm = x_ref.shape[0]
    h = _rmsnorm(x_ref[...], gmix_ref[...])
    h_hi = h.astype(BF16)
    h_lo = (h - h_hi.astype(F32)).astype(BF16)
    zm = _dot(h_hi, wmain_ref[...])
    wih = wih_ref[...]
    zi = (_dot(h_hi, wih) + _dot(h_lo, wih)) + _dot(h_hi, wil_ref[...])

    gb = zm[:, 0:d_conv]
    u = zm[:, d_conv:2 * d_conv] * zm[:, 2 * d_conv:3 * d_conv]

    @pl.when(i % tiles_per_batch == 0)
    def _():
        ubuf_ref[0:SUBLANES, :] = cprev_ref[0]

    ubuf_ref[SUBLANES:SUBLANES + tm, :] = u
    w = convw_ref[...]
    conv = (ubuf_ref[SUBLANES - 2:SUBLANES - 2 + tm, :] * w[0:1, :]
            + ubuf_ref[SUBLANES - 1:SUBLANES - 1 + tm, :] * w[1:2, :]) + u * w[2:3, :]
    convout_ref[...] = (gb * conv).astype(BF16)
    tail = u[tm - SUBLANES:tm, :]
    ulast_ref[0] = tail
    ubuf_ref[0:SUBLANES, :] = tail

    cos = cos_ref[...]
    slo = slo_ref[...]
    shi = shi_ref[...]
    bd = bd_ref[...]
    q0 = 3 * d_conv
    nq = N_HEADS * HEAD_DIM
    scale = HEAD_DIM ** -0.5 * LOG2E
    for c in range(nq // LANES):
        qc = zm[:, q0 + c * LANES:q0 + (c + 1) * LANES]
        qn = _head_norm(qc, gq_ref[...], bd)
        q_ref[:, c * LANES:(c + 1) * LANES] = (_rope(qn, cos, slo, shi) * scale).astype(BF16)
    k0 = q0 + nq
    kn = _rope(_head_norm(zm[:, k0:k0 + LANES], gk_ref[...], bd), cos, slo, shi)
    k_ref[...] = kn
    kb_ref[...] = kn.astype(BF16)
    v_ref[...] = zm[:, k0 + LANES:k0 + 2 * LANES]

    ni = N_IDX_HEADS * IDX_DIM
    for c in range(ni // LANES):
        qic = _rope(zi[:, c * LANES:(c + 1) * LANES], cos, slo, shi)
        qi3_ref[:, c * 4 * LANES:(c + 1) * 4 * LANES] = _stack_queries(qic)
    kw = zi[:, ni:ni + LANES]
    kin = _rope(_head_norm(kw, gik_ref[...], bd), cos, slo, shi)
    lane = lax.broadcasted_iota(I32, kw.shape, 1)
    kiwi_ref[...] = jnp.where(lane < IDX_DIM, kin, kw * (N_IDX_HEADS ** -0.5))
    ki3_ref[...] = _stack_keys(kin)


def _proj_call(x1, cprev, wts, tables, tm, tiles_per_batch):
    n, d = x1.shape
    (gmix, wmain, wih, wil, convw, gq, gk, gik, bd) = wts
    cos, slo, shi = tables
    d_conv = convw.shape[1]
    nt = n // tm
    row = lambda w: pl.BlockSpec((tm, w), lambda i: (i, 0))
    tab = pl.BlockSpec((tm, LANES), lambda i: (i % tiles_per_batch, 0))
    nq = N_HEADS * HEAD_DIM
    out_shape = (
        jax.ShapeDtypeStruct((n, d_conv), BF16),
        jax.ShapeDtypeStruct((n, nq), BF16),
        jax.ShapeDtypeStruct((n, LANES), F32),
        jax.ShapeDtypeStruct((n, LANES), F32),
        jax.ShapeDtypeStruct((n, LANES), BF16),
        jax.ShapeDtypeStruct((n, N_IDX_HEADS * 2 * LANES), BF16),
        jax.ShapeDtypeStruct((n, LANES), F32),
        jax.ShapeDtypeStruct((n, 2 * LANES), BF16),
        jax.ShapeDtypeStruct((nt, SUBLANES, d_conv), F32),
    )
    out_specs = (
        row(d_conv), row(nq), row(LANES), row(LANES), row(LANES), row(N_IDX_HEADS * 2 * LANES),
        row(LANES), row(2 * LANES),
        pl.BlockSpec((1, SUBLANES, d_conv), lambda i: (i, 0, 0)),
    )
    in_specs = [
        row(d),
        _const_spec(gmix.shape), _const_spec(wmain.shape), _const_spec(wih.shape), _const_spec(wil.shape),
        _const_spec(convw.shape),
        pl.BlockSpec((1, SUBLANES, d_conv), lambda i: (i // tiles_per_batch, 0, 0)),
        _const_spec(gq.shape), _const_spec(gk.shape), _const_spec(gik.shape),
        tab, tab, tab,
        _const_spec(bd.shape),
    ]
    return pl.pallas_call(
        functools.partial(_proj_kernel, tiles_per_batch=tiles_per_batch, d_conv=d_conv),
        grid=(nt,),
        in_specs=in_specs,
        out_specs=out_specs,
        out_shape=out_shape,
        scratch_shapes=[pltpu.VMEM((tm + SUBLANES, d_conv), F32)],
        compiler_params=pltpu.CompilerParams(
            dimension_semantics=("arbitrary",), vmem_limit_bytes=VMEM_LIMIT),
        name="proj",
    )(x1, gmix, wmain, wih, wil, convw, cprev, gq, gk, gik, cos, slo, shi, bd)


def _stack_keys_kernel(k_ref, o_ref):
    o_ref[...] = _stack_keys(k_ref[...])


def _stack_keys_call(kpad, tm):
    n = kpad.shape[0]
    return pl.pallas_call(
        _stack_keys_kernel,
        grid=(n // tm,),
        in_specs=[pl.BlockSpec((tm, LANES), lambda i: (i, 0))],
        out_specs=pl.BlockSpec((tm, 2 * LANES), lambda i: (i, 0)),
        out_shape=jax.ShapeDtypeStruct((n, 2 * LANES), BF16),
        name="stack_keys",
    )(kpad)


def _attn_kernel(qi3_ref, wit_ref, q_ref, ki3_ref, kb_ref, vt_ref, o_ref,
                 score_ref, bias_ref, ot_ref, m_ref, l_ref, s_ref, *, kc, past, n_keys, n_sel, idx_bits):
    tq = q_ref.shape[0]
    j = pl.program_id(1)
    q_first = past + j * tq
    pos = q_first + lax.broadcasted_iota(I32, (1, tq), 1)
    limit = jnp.minimum(((pos >> 6) + 1) * CHUNK, n_keys)
    max_limit = jnp.minimum((((q_first + tq - 1) >> 6) + 1) * CHUNK, n_keys)
    nk = (max_limit + kc - 1) // kc

    def rows(c):
        return pl.ds(pl.multiple_of(c * kc, kc), kc)

    def key_pos(c):
        return c * kc + lax.broadcasted_iota(I32, (kc, tq), 0)

    w8 = wit_ref[0] * (IDX_DIM ** -0.5)

    def score_body(c, carry):
        kk = ki3_ref[rows(c), :]
        for hh in range(N_IDX_HEADS):
            d = _dot_nt(kk, qi3_ref[:, hh * 2 * LANES:(hh + 1) * 2 * LANES])
            term = jnp.maximum(d, 0.0) * w8[hh:hh + 1, :]
            score_ref[rows(c), :] = term if hh == 0 else score_ref[rows(c), :] + term
        score_ref[rows(c), :] = jnp.where(key_pos(c) < limit, score_ref[rows(c), :] + 0.0, -jnp.inf)
        return carry

    lax.fori_loop(0, nk, score_body, 0)

    def count(pred):
        def body(c, acc):
            m = pred(c, score_ref[rows(c), :]).astype(I32)
            return acc + m.reshape(kc // SUBLANES, SUBLANES, tq).sum(axis=0)
        acc = lax.fori_loop(0, nk, body, jnp.zeros((SUBLANES, tq), I32))
        return acc.sum(axis=0, keepdims=True)

    def as_float(t):
        return pltpu.bitcast(t ^ ((t >> 31) & 0x7FFFFFFF), F32)

    c_nonneg = count(lambda c, s: s >= 0.0)
    t0 = jnp.where(c_nonneg >= n_sel, 0, INT_MIN).astype(I32)

    def bit_body(b, t):
        cand = t | (jnp.int32(1) << (30 - b))
        cand_f = as_float(cand)
        cnt = count(lambda c, s: s >= cand_f)
        return jnp.where(cnt >= n_sel, cand, t)

    t_sel = lax.fori_loop(0, 31, bit_body, t0)
    live = t_sel != INT_MIN
    thr = jnp.where(live, as_float(t_sel), -jnp.inf)

    c_gt = count(lambda c, s: s > thr)
    c_ge = count(lambda c, s: s >= thr)
    room = n_sel - c_gt
    surplus = jnp.max(jnp.where(live & (c_ge > n_sel), 1, 0))

    def tie_search():
        def body(b, jm):
            cand = jm | (jnp.int32(1) << (idx_bits - 1 - b))
            cnt = count(lambda c, s: (s == thr) & (key_pos(c) < cand))
            return jnp.where(cnt < room, cand, jm)
        return lax.fori_loop(0, idx_bits, body, jnp.zeros((1, tq), I32))

    jm = lax.cond(surplus > 0, tie_search, lambda: jnp.full((1, tq), 2 ** idx_bits, I32))
    jm = jnp.where(live, jm, -1)

    def bias_body(c, carry):
        s = score_ref[rows(c), :]
        sel = (s > thr) | ((s == thr) & (key_pos(c) <= jm))
        bias_ref[rows(c), :] = jnp.where(sel, 0.0, NEG_BIG)
        return carry

    lax.fori_loop(0, nk, bias_body, 0)

    group = N_HEADS // N_KV_HEADS
    m_ref[...] = jnp.full(m_ref.shape, NEG_BIG, F32)
    l_ref[...] = jnp.zeros(l_ref.shape, F32)
    ot_ref[...] = jnp.zeros(ot_ref.shape, F32)

    def fold(x, op):
        return op(x.reshape(kc // SUBLANES, SUBLANES, tq), axis=0)

    def logit_body(c, carry):
        for hh in range(N_HEADS):
            kv = hh // group
            part = slice(hh * SUBLANES, (hh + 1) * SUBLANES)
            kk = kb_ref[rows(c), kv * HEAD_DIM:(kv + 1) * HEAD_DIM]
            s = _dot_nt(kk, q_ref[:, hh * HEAD_DIM:(hh + 1) * HEAD_DIM]) + bias_ref[rows(c), :]
            s_ref[hh, rows(c), :] = s
            m_ref[part, :] = jnp.maximum(m_ref[part, :], fold(s, jnp.max))
        return carry

    lax.fori_loop(0, nk, logit_body, 0)

    def pv_body(c, carry):
        for hh in range(N_HEADS):
            kv = hh // group
            part = slice(hh * SUBLANES, (hh + 1) * SUBLANES)
            head = slice(hh * HEAD_DIM, (hh + 1) * HEAD_DIM)
            m = m_ref[part, :].max(axis=0, keepdims=True)
            p = jnp.exp2(s_ref[hh, rows(c), :] - m)
            l_ref[part, :] = l_ref[part, :] + fold(p, jnp.sum)
            vt = vt_ref[c, kv * HEAD_DIM:(kv + 1) * HEAD_DIM, :]
            ot_ref[head, :] = ot_ref[head, :] + _dot(vt, p.astype(BF16))
        return carry

    lax.fori_loop(0, nk, pv_body, 0)
    for hh in range(N_HEADS):
        part = slice(hh * SUBLANES, (hh + 1) * SUBLANES)
        head = slice(hh * HEAD_DIM, (hh + 1) * HEAD_DIM)
        ot_ref[head, :] = ot_ref[head, :] / l_ref[part, :].sum(axis=0, keepdims=True)
    o_ref[...] = ot_ref[...].T.astype(BF16)


def _attn_call(qi3, wit, q, ki3, kb, vt, *, batch, tq, kc, past, n_keys, n_sel):
    tq_total = q.shape[0] // batch
    nq = tq_total // tq
    lp = ki3.shape[1]
    idx_bits = max(1, int(np.ceil(np.log2(lp))))
    d_attn = q.shape[1]
    kernel = functools.partial(_attn_kernel, kc=kc, past=past, n_keys=n_keys, n_sel=n_sel, idx_bits=idx_bits)
    return pl.pallas_call(
        kernel,
        grid=(batch, nq),
        in_specs=[
            pl.BlockSpec((tq, qi3.shape[1]), lambda b, j: (b * nq + j, 0)),
            pl.BlockSpec((1, N_IDX_HEADS, tq), lambda b, j: (b, 0, j)),
            pl.BlockSpec((tq, d_attn), lambda b, j: (b * nq + j, 0)),
            pl.BlockSpec((None, lp, ki3.shape[2]), lambda b, j: (b, 0, 0)),
            pl.BlockSpec((None, lp, kb.shape[2]), lambda b, j: (b, 0, 0)),
            pl.BlockSpec((None, lp // kc, vt.shape[2], kc), lambda b, j: (b, 0, 0, 0)),
        ],
        out_specs=pl.BlockSpec((tq, d_attn), lambda b, j: (b * nq + j, 0)),
        out_shape=jax.ShapeDtypeStruct((batch * tq_total, d_attn), BF16),
        scratch_shapes=[
            pltpu.VMEM((lp, tq), F32),
            pltpu.VMEM((lp, tq), F32),
            pltpu.VMEM((d_attn, tq), F32),
            pltpu.VMEM((N_HEADS * SUBLANES, tq), F32),
            pltpu.VMEM((N_HEADS * SUBLANES, tq), F32),
            pltpu.VMEM((N_HEADS, lp, tq), F32),
        ],
        compiler_params=pltpu.CompilerParams(
            dimension_semantics=("arbitrary", "arbitrary"), vmem_limit_bytes=VMEM_LIMIT),
        name="attn",
    )(qi3, wit, q, ki3, kb, vt)


def _rope_tables(pos):
    half = ROT_DIM // 2
    inv = ROPE_THETA ** (-np.arange(half, dtype=np.float64) * (2.0 / ROT_DIM))
    ang = np.asarray(pos, np.float64)[:, None] * inv[None, :]
    cos = np.ones((len(pos), HEAD_DIM))
    slo = np.zeros((len(pos), HEAD_DIM))
    shi = np.zeros((len(pos), HEAD_DIM))
    cos[:, :half] = np.cos(ang)
    cos[:, half:ROT_DIM] = np.cos(ang)
    slo[:, :half] = -np.sin(ang)
    shi[:, half:ROT_DIM] = np.sin(ang)
    rep = LANES // HEAD_DIM
    return tuple(jnp.asarray(np.tile(t, (1, rep)), F32) for t in (cos, slo, shi))


def _pad_rows(a, rows):
    return jnp.pad(a, ((0, 0), (0, rows - a.shape[1])) + ((0, 0),) * (a.ndim - 2))


def _layer(x, past, cache, wts, *, tm_ffn, tm_proj, tq, kc):
    (ffn1, projw, woc, woa, ffn2) = wts
    b, t, d = x.shape
    n = b * t
    cache_k, cache_v, cache_ik, conv_prev = cache
    x1 = _ffn_call(x.reshape(n, d), *ffn1, tm=tm_ffn)

    d_conv = conv_prev.shape[-1]
    cprev = jnp.pad(conv_prev, ((0, 0), (SUBLANES - (CONV_W - 1), 0), (0, 0)))
    tables = _rope_tables(past + np.arange(t))
    conv_out, q, k, v, kb, qi3, kiwi, ki3, ulast = _proj_call(
        x1, cprev, projw, tables, tm_proj, t // tm_proj)

    n_keys = past + t
    lp = -(-n_keys // kc) * kc
    kb_all = kb.reshape(b, t, LANES)
    v_all = v.reshape(b, t, LANES).astype(BF16)
    ki3_all = ki3.reshape(b, t, 2 * LANES)
    if past:
        ck = cache_k.reshape(b, past, LANES).astype(BF16)
        cv = cache_v.reshape(b, past, LANES).astype(BF16)
        cik = jnp.pad(cache_ik, ((0, 0), (0, 0), (0, LANES - IDX_DIM))).reshape(b * past, LANES)
        cik3 = _stack_keys_call(cik, past).reshape(b, past, 2 * LANES)
        kb_all = jnp.concatenate([ck, kb_all], axis=1)
        v_all = jnp.concatenate([cv, v_all], axis=1)
        ki3_all = jnp.concatenate([cik3, ki3_all], axis=1)
    kb_all = _pad_rows(kb_all, lp)
    ki3_all = _pad_rows(ki3_all, lp)
    vt = jnp.swapaxes(_pad_rows(v_all, lp).reshape(b, lp // kc, kc, LANES), 2, 3)

    tqp = -(-t // tq) * tq
    wi = kiwi[:, IDX_DIM:IDX_DIM + N_IDX_HEADS].reshape(b, t, N_IDX_HEADS)
    wit = jnp.swapaxes(_pad_rows(wi, tqp), 1, 2)
    if tqp == t:
        padq = lambda a: a
    else:
        padq = lambda a: _pad_rows(a.reshape(b, t, a.shape[-1]), tqp).reshape(b * tqp, a.shape[-1])
    n_sel = max(1, min(TOP_K, n_keys // 4))
    attn = _attn_call(padq(qi3), wit, padq(q), ki3_all, kb_all, vt,
                      batch=b, tq=tq, kc=kc, past=past, n_keys=n_keys, n_sel=n_sel)
    if tqp != t:
        attn = attn.reshape(b, tqp, -1)[:, :t].reshape(n, -1)

    y = _out_ffn_call(x1, conv_out, attn, woc, woa, *ffn2, tm=tm_ffn)

    tiles = t // tm_proj
    conv_state = ulast.reshape(b, tiles, SUBLANES, d_conv)[:, -1, SUBLANES - (CONV_W - 1):]
    return (y.reshape(b, t, d),
            k.reshape(b, t, N_KV_HEADS, HEAD_DIM),
            v.reshape(b, t, N_KV_HEADS, HEAD_DIM),
            kiwi[:, :IDX_DIM].reshape(b, t, IDX_DIM),
            conv_state)


def _prep_weights(l, ffn1_norm, ffn1_w_gate, ffn1_w_up, ffn1_w_down, mix_norm, w_in, conv_w,
                  q_norm, k_norm, idx_k_norm, w_out, ffn2_norm, ffn2_w_gate, ffn2_w_up, ffn2_w_down):
    d_conv = conv_w.shape[-1]
    n_main = 3 * d_conv + N_HEADS * HEAD_DIM + 2 * N_KV_HEADS * HEAD_DIM
    row = lambda g: g[l][None, :]
    ffn1 = (row(ffn1_norm), ffn1_w_gate[l].astype(BF16), ffn1_w_up[l].astype(BF16), ffn1_w_down[l].astype(BF16))
    ffn2 = (row(ffn2_norm), ffn2_w_gate[l].astype(BF16), ffn2_w_up[l].astype(BF16), ffn2_w_down[l].astype(BF16))
    w = w_in[l]
    wmain = w[:, :n_main].astype(BF16)
    widx = w[:, n_main:]
    widx = jnp.pad(widx, ((0, 0), (0, -widx.shape[1] % LANES)))
    wih = widx.astype(BF16)
    wil = (widx - wih.astype(F32)).astype(BF16)
    gq = jnp.tile(q_norm[l], LANES // HEAD_DIM)[None, :]
    gk = jnp.tile(k_norm[l], LANES // HEAD_DIM)[None, :]
    gik = jnp.pad(idx_k_norm[l], (0, LANES - IDX_DIM))[None, :]
    lane = np.arange(LANES)
    bd = jnp.asarray(lane[:, None] // HEAD_DIM == lane[None, :] // HEAD_DIM, BF16)
    projw = (row(mix_norm), wmain, wih, wil, conv_w[l], gq, gk, gik, bd)
    wo = w_out[l].astype(BF16)
    return (ffn1, projw, wo[:d_conv], wo[d_conv:], ffn2)


def kernel(x_prompt, x_sample, cache_k, cache_v, cache_idx_k, state_conv, ffn1_norm, ffn1_w_gate, ffn1_w_up, ffn1_w_down, mix_norm, w_in, conv_w, q_norm, k_norm, idx_k_norm, w_out, ffn2_norm, ffn2_w_gate, ffn2_w_up, ffn2_w_down):
    depth = w_in.shape[0]
    bp, tp, _ = x_prompt.shape
    bs, ts, _ = x_sample.shape
    past = cache_k.shape[2]
    d_conv = conv_w.shape[-1]
    hp, hs = x_prompt, x_sample
    outs_p, outs_s = [], []
    for l in range(depth):
        wts = _prep_weights(l, ffn1_norm, ffn1_w_gate, ffn1_w_up, ffn1_w_down, mix_norm, w_in, conv_w,
                            q_norm, k_norm, idx_k_norm, w_out, ffn2_norm, ffn2_w_gate, ffn2_w_up, ffn2_w_down)
        empty = (None, None, None, jnp.zeros((bp, CONV_W - 1, d_conv), F32))
        hp, *rest_p = _layer(hp, 0, empty, wts, tm_ffn=256, tm_proj=256, tq=256, kc=256)
        cache = (cache_k[l], cache_v[l], cache_idx_k[l], state_conv[l])
        hs, *rest_s = _layer(hs, past, cache, wts, tm_ffn=bs * ts, tm_proj=ts, tq=128, kc=256)
        outs_p.append(rest_p)
        outs_s.append(rest_s)
    stack = lambda outs, i: jnp.stack([o[i] for o in outs])
    return (hp, hs,
            stack(outs_p, 0), stack(outs_p, 1), stack(outs_p, 2), stack(outs_p, 3),
            stack(outs_s, 0), stack(outs_s, 1), stack(outs_s, 2), stack(outs_s, 3))
```

```python
import functools

import numpy as np
import jax
import jax.numpy as jnp
from jax import lax
from jax.experimental import pallas as pl
from jax.experimental.pallas import tpu as pltpu

F32 = jnp.float32
BF16 = jnp.bfloat16
I32 = jnp.int32

CHUNK = 64
CONV_W = 3
N_HEADS = 8
N_KV_HEADS = 2
HEAD_DIM = 64
ROT_DIM = 16
ROPE_THETA = 500000.0
N_IDX_HEADS = 8
IDX_DIM = 64
TOP_K = 256
EPS = 1e-6

LANES = 128
SUBLANES = 8
INT_MIN = -(2 ** 31)
LOG2E = 1.4426950408889634
NEG_BIG = -1e30
VMEM_LIMIT = 56 * 1024 * 1024


def _const_spec(shape):
    nd = len(shape)
    return pl.BlockSpec(shape, lambda *_: (0,) * nd, pipeline_mode=pl.Buffered(1))


def _dot(a, b):
    return jnp.dot(a, b, preferred_element_type=F32)


def _dot_nt(a, b):
    return lax.dot_general(a, b, (((1,), (1,)), ((), ())), preferred_element_type=F32)


def _split_bf16(x):
    hi = x.astype(BF16).astype(F32)
    lo = (x - hi).astype(BF16).astype(F32)
    return hi, lo


def _rmsnorm(x, g):
    ms = jnp.mean(x * x, axis=-1, keepdims=True)
    return (x * lax.rsqrt(ms + EPS)) * g


def _swiglu(h, wg_ref, wu_ref, wd_ref):
    g = _dot(h, wg_ref[...])
    u = _dot(h, wu_ref[...])
    a = (g * jax.nn.sigmoid(g)) * u
    return _dot(a.astype(BF16), wd_ref[...])


def _ffn_kernel(x_ref, g_ref, wg_ref, wu_ref, wd_ref, o_ref):
    x = x_ref[...]
    h = _rmsnorm(x, g_ref[...]).astype(BF16)
    o_ref[...] = x + 0.5 * _swiglu(h, wg_ref, wu_ref, wd_ref)


def _ffn_call(x, g, wg, wu, wd, tm):
    n, d = x.shape
    dff = wg.shape[1]
    return pl.pallas_call(
        _ffn_kernel,
        grid=(n // tm,),
        in_specs=[
            pl.BlockSpec((tm, d), lambda i: (i, 0)),
            _const_spec((1, d)),
            _const_spec((d, dff)),
            _const_spec((d, dff)),
            _const_spec((dff, d)),
        ],
        out_specs=pl.BlockSpec((tm, d), lambda i: (i, 0)),
        out_shape=jax.ShapeDtypeStruct((n, d), F32),
        compiler_params=pltpu.CompilerParams(
            dimension_semantics=("arbitrary",), vmem_limit_bytes=VMEM_LIMIT),
        name="ffn",
    )(x, g, wg, wu, wd)


def _out_ffn_kernel(x_ref, c_ref, a_ref, woc_ref, woa_ref, g_ref, wg_ref, wu_ref, wd_ref, o_ref):
    x = x_ref[...] + (_dot(c_ref[...], woc_ref[...]) + _dot(a_ref[...], woa_ref[...]))
    h = _rmsnorm(x, g_ref[...]).astype(BF16)
    o_ref[...] = x + 0.5 * _swiglu(h, wg_ref, wu_ref, wd_ref)


def _out_ffn_call(x, conv_out, attn, woc, woa, g, wg, wu, wd, tm):
    n, d = x.shape
    dff = wg.shape[1]
    dc = conv_out.shape[1]
    da = attn.shape[1]
    return pl.pallas_call(
        _out_ffn_kernel,
        grid=(n // tm,),
        in_specs=[
            pl.BlockSpec((tm, d), lambda i: (i, 0)),
            pl.BlockSpec((tm, dc), lambda i: (i, 0)),
            pl.BlockSpec((tm, da), lambda i: (i, 0)),
            _const_spec((dc, d)),
            _const_spec((da, d)),
            _const_spec((1, d)),
            _const_spec((d, dff)),
            _const_spec((d, dff)),
            _const_spec((dff, d)),
        ],
        out_specs=pl.BlockSpec((tm, d), lambda i: (i, 0)),
        out_shape=jax.ShapeDtypeStruct((n, d), F32),
        compiler_params=pltpu.CompilerParams(
            dimension_semantics=("arbitrary",), vmem_limit_bytes=VMEM_LIMIT),
        name="out_ffn",
    )(x, conv_out, attn, woc, woa, g, wg, wu, wd)


def _rope(y, cos, sin_lo, sin_hi):
    half = ROT_DIM // 2
    return (y * cos + pltpu.roll(y, LANES - half, 1) * sin_lo) + pltpu.roll(y, half, 1) * sin_hi


def _group_sumsq(x, bd):
    hi, lo = _split_bf16(x * x)
    return _dot(hi.astype(BF16), bd) + _dot(lo.astype(BF16), bd)


def _head_norm(x, g, bd):
    ms = _group_sumsq(x, bd) * (1.0 / HEAD_DIM)
    return (x * lax.rsqrt(ms + EPS)) * g


def _stack_keys(kin):
    lane = lax.broadcasted_iota(I32, kin.shape, 1)
    first = lane < IDX_DIM
    hi, lo = _split_bf16(kin)
    c0 = jnp.where(first, hi, pltpu.roll(hi, IDX_DIM, 1))
    c1 = jnp.where(first, lo, 0.0)
    return jnp.concatenate([c0, c1], axis=1).astype(BF16)


def _stack_queries(qc):
    lane = lax.broadcasted_iota(I32, qc.shape, 1)
    first = lane < IDX_DIM
    hi, lo = _split_bf16(qc)
    hi_sw = pltpu.roll(hi, IDX_DIM, 1)
    lo_sw = pltpu.roll(lo, IDX_DIM, 1)
    a0 = jnp.where(first, hi, lo_sw)
    a1 = jnp.where(first, hi, 0.0)
    b0 = jnp.where(first, hi_sw, lo)
    b1 = jnp.where(first, hi_sw, 0.0)
    return jnp.concatenate([a0, a1, b0, b1], axis=1).astype(BF16)


def _proj_kernel(x_ref, gmix_ref, wmain_ref, wih_ref, wil_ref, convw_ref, cprev_ref,
                 gq_ref, gk_ref, gik_ref, cos_ref, slo_ref, shi_ref, bd_ref,
                 convout_ref, q_ref, k_ref, v_ref, kb_ref, qi3_ref, kiwi_ref, ki3_ref, ulast_ref,
                 ubuf_ref, *, tiles_per_batch, d_conv):
    i = pl.program_id(0)
    tm = x_ref.shape[0]

    @pl.when(i % tiles_per_batch == 0)
    def _():
        ubuf_ref[0:SUBLANES, :] = cprev_ref[0]

    h = _rmsnorm(x_ref[...], gmix_ref[...])
    h_hi = h.astype(BF16)
    h_lo = (h - h_hi.astype(F32)).astype(BF16)
    q0 = 3 * d_conv
    nq = N_HEADS * HEAD_DIM
    n_main = wmain_ref.shape[1]
    zq = _dot(h_hi, wmain_ref[:, q0:n_main])
    wih = wih_ref[...]
    zi = (_dot(h_hi, wih) + _dot(h_lo, wih)) + _dot(h_hi, wil_ref[...])

    cos = cos_ref[...]
    slo = slo_ref[...]
    shi = shi_ref[...]
    bd = bd_ref[...]
    scale = HEAD_DIM ** -0.5 * LOG2E
    for c in range(nq // LANES):
        qc = zq[:, c * LANES:(c + 1) * LANES]
        qn = _head_norm(qc, gq_ref[...], bd)
        q_ref[:, c * LANES:(c + 1) * LANES] = (_rope(qn, cos, slo, shi) * scale).astype(BF16)
    kn = _rope(_head_norm(zq[:, nq:nq + LANES], gk_ref[...], bd), cos, slo, shi)
    k_ref[...] = kn
    kb_ref[...] = kn.astype(BF16)
    v_ref[...] = zq[:, nq + LANES:nq + 2 * LANES]

    zc = _dot(h_hi, wmain_ref[:, 0:q0])

    ni = N_IDX_HEADS * IDX_DIM
    for c in range(ni // LANES):
        qic = _rope(zi[:, c * LANES:(c + 1) * LANES], cos, slo, shi)
        qi3_ref[:, c * 4 * LANES:(c + 1) * 4 * LANES] = _stack_queries(qic)
    kw = zi[:, ni:ni + LANES]
    kin = _rope(_head_norm(kw, gik_ref[...], bd), cos, slo, shi)
    lane = lax.broadcasted_iota(I32, kw.shape, 1)
    kiwi_ref[...] = jnp.where(lane < IDX_DIM, kin, kw * (N_IDX_HEADS ** -0.5))
    ki3_ref[...] = _stack_keys(kin)

    gb = zc[:, 0:d_conv]
    u = zc[:, d_conv:2 * d_conv] * zc[:, 2 * d_conv:3 * d_conv]
    ubuf_ref[SUBLANES:SUBLANES + tm, :] = u
    w = convw_ref[...]
    conv = (ubuf_ref[SUBLANES - 2:SUBLANES - 2 + tm, :] * w[0:1, :]
            + ubuf_ref[SUBLANES - 1:SUBLANES - 1 + tm, :] * w[1:2, :]) + u * w[2:3, :]
    convout_ref[...] = (gb * conv).astype(BF16)
    tail = u[tm - SUBLANES:tm, :]
    ulast_ref[0] = tail
    ubuf_ref[0:SUBLANES, :] = tail


def _proj_call(x1, cprev, wts, tables, tm, tiles_per_batch):
    n, d = x1.shape
    (gmix, wmain, wih, wil, convw, gq, gk, gik, bd) = wts
    cos, slo, shi = tables
    d_conv = convw.shape[1]
    nt = n // tm
    row = lambda w: pl.BlockSpec((tm, w), lambda i: (i, 0))
    tab = pl.BlockSpec((tm, LANES), lambda i: (i % tiles_per_batch, 0))
    nq = N_HEADS * HEAD_DIM
    out_shape = (
        jax.ShapeDtypeStruct((n, d_conv), BF16),
        jax.ShapeDtypeStruct((n, nq), BF16),
        jax.ShapeDtypeStruct((n, LANES), F32),
        jax.ShapeDtypeStruct((n, LANES), F32),
        jax.ShapeDtypeStruct((n, LANES), BF16),
        jax.ShapeDtypeStruct((n, N_IDX_HEADS * 2 * LANES), BF16),
        jax.ShapeDtypeStruct((n, LANES), F32),
        jax.ShapeDtypeStruct((n, 2 * LANES), BF16),
        jax.ShapeDtypeStruct((nt, SUBLANES, d_conv), F32),
    )
    out_specs = (
        row(d_conv), row(nq), row(LANES), row(LANES), row(LANES), row(N_IDX_HEADS * 2 * LANES),
        row(LANES), row(2 * LANES),
        pl.BlockSpec((1, SUBLANES, d_conv), lambda i: (i, 0, 0)),
    )
    in_specs = [
        row(d),
        _const_spec(gmix.shape), _const_spec(wmain.shape), _const_spec(wih.shape), _const_spec(wil.shape),
        _const_spec(convw.shape),
        pl.BlockSpec((1, SUBLANES, d_conv), lambda i: (i // tiles_per_batch, 0, 0)),
        _const_spec(gq.shape), _const_spec(gk.shape), _const_spec(gik.shape),
        tab, tab, tab,
        _const_spec(bd.shape),
    ]
    return pl.pallas_call(
        functools.partial(_proj_kernel, tiles_per_batch=tiles_per_batch, d_conv=d_conv),
        grid=(nt,),
        in_specs=in_specs,
        out_specs=out_specs,
        out_shape=out_shape,
        scratch_shapes=[pltpu.VMEM((tm + SUBLANES, d_conv), F32)],
        compiler_params=pltpu.CompilerParams(
            dimension_semantics=("arbitrary",), vmem_limit_bytes=VMEM_LIMIT),
        name="proj",
    )(x1, gmix, wmain, wih, wil, convw, cprev, gq, gk, gik, cos, slo, shi, bd)


def _stack_keys_kernel(k_ref, o_ref):
    o_ref[...] = _stack_keys(k_ref[...])


def _stack_keys_call(kpad, tm):
    n = kpad.shape[0]
    return pl.pallas_call(
        _stack_keys_kernel,
        grid=(n // tm,),
        in_specs=[pl.BlockSpec((tm, LANES), lambda i: (i, 0))],
        out_specs=pl.BlockSpec((tm, 2 * LANES), lambda i: (i, 0)),
        out_shape=jax.ShapeDtypeStruct((n, 2 * LANES), BF16),
        name="stack_keys",
    )(kpad)


def _attn_kernel(qi3_ref, wit_ref, q_ref, ki3_ref, kb_ref, vt_ref, o_ref,
                 score_ref, bias_ref, ot_ref, m_ref, l_ref, s_ref, *, kc, past, n_keys, n_sel, idx_bits):
    tq = q_ref.shape[0]
    j = pl.program_id(1)
    q_first = past + j * tq
    pos = q_first + lax.broadcasted_iota(I32, (1, tq), 1)
    limit = jnp.minimum(((pos >> 6) + 1) * CHUNK, n_keys)
    max_limit = jnp.minimum((((q_first + tq - 1) >> 6) + 1) * CHUNK, n_keys)
    nk = (max_limit + kc - 1) // kc

    def rows(c):
        return pl.ds(pl.multiple_of(c * kc, kc), kc)

    def key_pos(c):
        return c * kc + lax.broadcasted_iota(I32, (kc, tq), 0)

    w8 = wit_ref[0] * (IDX_DIM ** -0.5)

    def score_body(c, carry):
        kk = ki3_ref[rows(c), :]
        for hh in range(N_IDX_HEADS):
            d = _dot_nt(kk, qi3_ref[:, hh * 2 * LANES:(hh + 1) * 2 * LANES])
            term = jnp.maximum(d, 0.0) * w8[hh:hh + 1, :]
            score_ref[rows(c), :] = term if hh == 0 else score_ref[rows(c), :] + term
        score_ref[rows(c), :] = jnp.where(key_pos(c) < limit, score_ref[rows(c), :] + 0.0, -jnp.inf)
        return carry

    lax.fori_loop(0, nk, score_body, 0)

    def count(pred):
        def body(c, acc):
            m = pred(c, score_ref[rows(c), :]).astype(I32)
            return acc + m.reshape(kc // SUBLANES, SUBLANES, tq).sum(axis=0)
        acc = lax.fori_loop(0, nk, body, jnp.zeros((SUBLANES, tq), I32))
        return acc.sum(axis=0, keepdims=True)

    def as_float(t):
        return pltpu.bitcast(t ^ ((t >> 31) & 0x7FFFFFFF), F32)

    c_nonneg = count(lambda c, s: s >= 0.0)
    t0 = jnp.where(c_nonneg >= n_sel, 0, INT_MIN).astype(I32)

    def bit_body(b, t):
        cand = t | (jnp.int32(1) << (30 - b))
        cand_f = as_float(cand)
        cnt = count(lambda c, s: s >= cand_f)
        return jnp.where(cnt >= n_sel, cand, t)

    t_sel = lax.fori_loop(0, 31, bit_body, t0)
    live = t_sel != INT_MIN
    thr = jnp.where(live, as_float(t_sel), -jnp.inf)

    c_gt = count(lambda c, s: s > thr)
    c_ge = count(lambda c, s: s >= thr)
    room = n_sel - c_gt
    surplus = jnp.max(jnp.where(live & (c_ge > n_sel), 1, 0))

    def tie_search():
        def body(b, jm):
            cand = jm | (jnp.int32(1) << (idx_bits - 1 - b))
            cnt = count(lambda c, s: (s == thr) & (key_pos(c) < cand))
            return jnp.where(cnt < room, cand, jm)
        return lax.fori_loop(0, idx_bits, body, jnp.zeros((1, tq), I32))

    jm = lax.cond(surplus > 0, tie_search, lambda: jnp.full((1, tq), 2 ** idx_bits, I32))
    jm = jnp.where(live, jm, -1)

    def bias_body(c, carry):
        s = score_ref[rows(c), :]
        sel = (s > thr) | ((s == thr) & (key_pos(c) <= jm))
        bias_ref[rows(c), :] = jnp.where(sel, 0.0, NEG_BIG)
        return carry

    lax.fori_loop(0, nk, bias_body, 0)

    group = N_HEADS // N_KV_HEADS
    m_ref[...] = jnp.full(m_ref.shape, NEG_BIG, F32)
    l_ref[...] = jnp.zeros(l_ref.shape, F32)
    ot_ref[...] = jnp.zeros(ot_ref.shape, F32)

    def fold(x, op):
        return op(x.reshape(kc // SUBLANES, SUBLANES, tq), axis=0)

    def logit_body(c, carry):
        for hh in range(N_HEADS):
            kv = hh // group
            part = slice(hh * SUBLANES, (hh + 1) * SUBLANES)
            kk = kb_ref[rows(c), kv * HEAD_DIM:(kv + 1) * HEAD_DIM]
            s = _dot_nt(kk, q_ref[:, hh * HEAD_DIM:(hh + 1) * HEAD_DIM]) + bias_ref[rows(c), :]
            s_ref[hh, rows(c), :] = s
            m_ref[part, :] = jnp.maximum(m_ref[part, :], fold(s, jnp.max))
        return carry

    lax.fori_loop(0, nk, logit_body, 0)

    def pv_body(c, carry):
        for hh in range(N_HEADS):
            kv = hh // group
            part = slice(hh * SUBLANES, (hh + 1) * SUBLANES)
            head = slice(hh * HEAD_DIM, (hh + 1) * HEAD_DIM)
            m = m_ref[part, :].max(axis=0, keepdims=True)
            p = jnp.exp2(s_ref[hh, rows(c), :] - m)
            l_ref[part, :] = l_ref[part, :] + fold(p, jnp.sum)
            vt = vt_ref[c, kv * HEAD_DIM:(kv + 1) * HEAD_DIM, :]
            ot_ref[head, :] = ot_ref[head, :] + _dot(vt, p.astype(BF16))
        return carry

    lax.fori_loop(0, nk, pv_body, 0)
    for hh in range(N_HEADS):
        part = slice(hh * SUBLANES, (hh + 1) * SUBLANES)
        head = slice(hh * HEAD_DIM, (hh + 1) * HEAD_DIM)
        ot_ref[head, :] = ot_ref[head, :] / l_ref[part, :].sum(axis=0, keepdims=True)
    o_ref[...] = ot_ref[...].T.astype(BF16)


def _attn_call(qi3, wit, q, ki3, kb, vt, *, batch, tq, kc, past, n_keys, n_sel):
    tq_total = q.shape[0] // batch
    nq = tq_total // tq
    lp = ki3.shape[1]
    idx_bits = max(1, int(np.ceil(np.log2(lp))))
    d_attn = q.shape[1]
    kernel = functools.partial(_attn_kernel, kc=kc, past=past, n_keys=n_keys, n_sel=n_sel, idx_bits=idx_bits)
    return pl.pallas_call(
        kernel,
        grid=(batch, nq),
        in_specs=[
            pl.BlockSpec((tq, qi3.shape[1]), lambda b, j: (b * nq + j, 0)),
            pl.BlockSpec((1, N_IDX_HEADS, tq), lambda b, j: (b, 0, j)),
            pl.BlockSpec((tq, d_attn), lambda b, j: (b * nq + j, 0)),
            pl.BlockSpec((None, lp, ki3.shape[2]), lambda b, j: (b, 0, 0)),
            pl.BlockSpec((None, lp, kb.shape[2]), lambda b, j: (b, 0, 0)),
            pl.BlockSpec((None, lp // kc, vt.shape[2], kc), lambda b, j: (b, 0, 0, 0)),
        ],
        out_specs=pl.BlockSpec((tq, d_attn), lambda b, j: (b * nq + j, 0)),
        out_shape=jax.ShapeDtypeStruct((batch * tq_total, d_attn), BF16),
        scratch_shapes=[
            pltpu.VMEM((lp, tq), F32),
            pltpu.VMEM((lp, tq), F32),
            pltpu.VMEM((d_attn, tq), F32),
            pltpu.VMEM((N_HEADS * SUBLANES, tq), F32),
            pltpu.VMEM((N_HEADS * SUBLANES, tq), F32),
            pltpu.VMEM((N_HEADS, lp, tq), F32),
        ],
        compiler_params=pltpu.CompilerParams(
            dimension_semantics=("arbitrary", "arbitrary"), vmem_limit_bytes=VMEM_LIMIT),
        name="attn",
    )(qi3, wit, q, ki3, kb, vt)


def _rope_tables(pos):
    half = ROT_DIM // 2
    inv = ROPE_THETA ** (-np.arange(half, dtype=np.float64) * (2.0 / ROT_DIM))
    ang = np.asarray(pos, np.float64)[:, None] * inv[None, :]
    cos = np.ones((len(pos), HEAD_DIM))
    slo = np.zeros((len(pos), HEAD_DIM))
    shi = np.zeros((len(pos), HEAD_DIM))
    cos[:, :half] = np.cos(ang)
    cos[:, half:ROT_DIM] = np.cos(ang)
    slo[:, :half] = -np.sin(ang)
    shi[:, half:ROT_DIM] = np.sin(ang)
    rep = LANES // HEAD_DIM
    return tuple(jnp.asarray(np.tile(t, (1, rep)), F32) for t in (cos, slo, shi))


def _pad_rows(a, rows):
    return jnp.pad(a, ((0, 0), (0, rows - a.shape[1])) + ((0, 0),) * (a.ndim - 2))


def _layer(x, past, cache, wts, *, tm_ffn, tm_proj, tq, kc):
    (ffn1, projw, woc, woa, ffn2) = wts
    b, t, d = x.shape
    n = b * t
    cache_k, cache_v, cache_ik, conv_prev = cache
    x1 = _ffn_call(x.reshape(n, d), *ffn1, tm=tm_ffn)

    d_conv = conv_prev.shape[-1]
    cprev = jnp.pad(conv_prev, ((0, 0), (SUBLANES - (CONV_W - 1), 0), (0, 0)))
    tables = _rope_tables(past + np.arange(t))
    conv_out, q, k, v, kb, qi3, kiwi, ki3, ulast = _proj_call(
        x1, cprev, projw, tables, tm_proj, t // tm_proj)

    n_keys = past + t
    lp = -(-n_keys // kc) * kc
    kb_all = kb.reshape(b, t, LANES)
    v_all = v.reshape(b, t, LANES).astype(BF16)
    ki3_all = ki3.reshape(b, t, 2 * LANES)
    if past:
        ck = cache_k.reshape(b, past, LANES).astype(BF16)
        cv = cache_v.reshape(b, past, LANES).astype(BF16)
        cik = jnp.pad(cache_ik, ((0, 0), (0, 0), (0, LANES - IDX_DIM))).reshape(b * past, LANES)
        cik3 = _stack_keys_call(cik, past).reshape(b, past, 2 * LANES)
        kb_all = jnp.concatenate([ck, kb_all], axis=1)
        v_all = jnp.concatenate([cv, v_all], axis=1)
        ki3_all = jnp.concatenate([cik3, ki3_all], axis=1)
    kb_all = _pad_rows(kb_all, lp)
    ki3_all = _pad_rows(ki3_all, lp)
    vt = jnp.swapaxes(_pad_rows(v_all, lp).reshape(b, lp // kc, kc, LANES), 2, 3)

    tqp = -(-t // tq) * tq
    wi = kiwi[:, IDX_DIM:IDX_DIM + N_IDX_HEADS].reshape(b, t, N_IDX_HEADS)
    wit = jnp.swapaxes(_pad_rows(wi, tqp), 1, 2)
    if tqp == t:
        padq = lambda a: a
    else:
        padq = lambda a: _pad_rows(a.reshape(b, t, a.shape[-1]), tqp).reshape(b * tqp, a.shape[-1])
    n_sel = max(1, min(TOP_K, n_keys // 4))
    attn = _attn_call(padq(qi3), wit, padq(q), ki3_all, kb_all, vt,
                      batch=b, tq=tq, kc=kc, past=past, n_keys=n_keys, n_sel=n_sel)
    if tqp != t:
        attn = attn.reshape(b, tqp, -1)[:, :t].reshape(n, -1)

    y = _out_ffn_call(x1, conv_out, attn, woc, woa, *ffn2, tm=tm_ffn)

    tiles = t // tm_proj
    conv_state = ulast.reshape(b, tiles, SUBLANES, d_conv)[:, -1, SUBLANES - (CONV_W - 1):]
    return (y.reshape(b, t, d),
            k.reshape(b, t, N_KV_HEADS, HEAD_DIM),
            v.reshape(b, t, N_KV_HEADS, HEAD_DIM),
            kiwi[:, :IDX_DIM].reshape(b, t, IDX_DIM),
            conv_state)


def _prep_weights(l, ffn1_norm, ffn1_w_gate, ffn1_w_up, ffn1_w_down, mix_norm, w_in, conv_w,
                  q_norm, k_norm, idx_k_norm, w_out, ffn2_norm, ffn2_w_gate, ffn2_w_up, ffn2_w_down):
    d_conv = conv_w.shape[-1]
    n_main = 3 * d_conv + N_HEADS * HEAD_DIM + 2 * N_KV_HEADS * HEAD_DIM
    row = lambda g: g[l][None, :]
    ffn1 = (row(ffn1_norm), ffn1_w_gate[l].astype(BF16), ffn1_w_up[l].astype(BF16), ffn1_w_down[l].astype(BF16))
    ffn2 = (row(ffn2_norm), ffn2_w_gate[l].astype(BF16), ffn2_w_up[l].astype(BF16), ffn2_w_down[l].astype(BF16))
    w = w_in[l]
    wmain = w[:, :n_main].astype(BF16)
    widx = w[:, n_main:]
    widx = jnp.pad(widx, ((0, 0), (0, -widx.shape[1] % LANES)))
    wih = widx.astype(BF16)
    wil = (widx - wih.astype(F32)).astype(BF16)
    gq = jnp.tile(q_norm[l], LANES // HEAD_DIM)[None, :]
    gk = jnp.tile(k_norm[l], LANES // HEAD_DIM)[None, :]
    gik = jnp.pad(idx_k_norm[l], (0, LANES - IDX_DIM))[None, :]
    lane = np.arange(LANES)
    bd = jnp.asarray(lane[:, None] // HEAD_DIM == lane[None, :] // HEAD_DIM, BF16)
    projw = (row(mix_norm), wmain, wih, wil, conv_w[l], gq, gk, gik, bd)
    wo = w_out[l].astype(BF16)
    return (ffn1, projw, wo[:d_conv], wo[d_conv:], ffn2)


def kernel(x_prompt, x_sample, cache_k, cache_v, cache_idx_k, state_conv, ffn1_norm, ffn1_w_gate, ffn1_w_up, ffn1_w_down, mix_norm, w_in, conv_w, q_norm, k_norm, idx_k_norm, w_out, ffn2_norm, ffn2_w_gate, ffn2_w_up, ffn2_w_down):
    depth = w_in.shape[0]
    bp, tp, _ = x_prompt.shape
    bs, ts, _ = x_sample.shape
    past = cache_k.shape[2]
    d_conv = conv_w.shape[-1]
    hp, hs = x_prompt, x_sample
    outs_p, outs_s = [], []
    for l in range(depth):
        wts = _prep_weights(l, ffn1_norm, ffn1_w_gate, ffn1_w_up, ffn1_w_down, mix_norm, w_in, conv_w,
                            q_norm, k_norm, idx_k_norm, w_out, ffn2_norm, ffn2_w_gate, ffn2_w_up, ffn2_w_down)
        empty = (None, None, None, jnp.zeros((bp, CONV_W - 1, d_conv), F32))
        hp, *rest_p = _layer(hp, 0, empty, wts, tm_ffn=256, tm_proj=256, tq=256, kc=256)
        cache = (cache_k[l], cache_v[l], cache_idx_k[l], state_conv[l])
        hs, *rest_s = _layer(hs, past, cache, wts, tm_ffn=bs * ts, tm_proj=ts, tq=128, kc=256)
        outs_p.append(rest_p)
        outs_s.append(rest_s)
    stack = lambda outs, i: jnp.stack([o[i] for o in outs])
    return (hp, hs,
            stack(outs_p, 0), stack(outs_p, 1), stack(outs_p, 2), stack(outs_p, 3),
            stack(outs_s, 0), stack(outs_s, 1), stack(outs_s, 2), stack(outs_s, 3))
```

```python
import functools

import numpy as np
import jax
import jax.numpy as jnp
from jax import lax
from jax.experimental import pallas as pl
from jax.experimental.pallas import tpu as pltpu

F32 = jnp.float32
BF16 = jnp.bfloat16
I32 = jnp.int32
I16 = jnp.int16

CHUNK = 64
CONV_W = 3
N_HEADS = 8
N_KV_HEADS = 2
HEAD_DIM = 64
ROT_DIM = 16
ROPE_THETA = 500000.0
N_IDX_HEADS = 8
IDX_DIM = 64
TOP_K = 256
EPS = 1e-6

LANES = 128
SUBLANES = 8
PACKED_ROWS = 16
INT_MIN = -(2 ** 31)
I16_MIN = -(2 ** 15)
LOG2E = 1.4426950408889634
NEG_BIG = -1e30
VMEM_LIMIT = 56 * 1024 * 1024


def _const_spec(shape):
    nd = len(shape)
    return pl.BlockSpec(shape, lambda *_: (0,) * nd, pipeline_mode=pl.Buffered(1))


def _dot(a, b):
    return jnp.dot(a, b, preferred_element_type=F32)


def _dot_nt(a, b):
    return lax.dot_general(a, b, (((1,), (1,)), ((), ())), preferred_element_type=F32)


def _split_bf16(x):
    hi = x.astype(BF16).astype(F32)
    lo = (x - hi).astype(BF16).astype(F32)
    return hi, lo


def _rmsnorm(x, g):
    ms = jnp.mean(x * x, axis=-1, keepdims=True)
    return (x * lax.rsqrt(ms + EPS)) * g


def _swiglu(h, wg_ref, wu_ref, wd_ref):
    g = _dot(h, wg_ref[...])
    u = _dot(h, wu_ref[...])
    a = (g * jax.nn.sigmoid(g)) * u
    return _dot(a.astype(BF16), wd_ref[...])


def _ffn_kernel(x_ref, g_ref, wg_ref, wu_ref, wd_ref, o_ref):
    x = x_ref[...]
    h = _rmsnorm(x, g_ref[...]).astype(BF16)
    o_ref[...] = x + 0.5 * _swiglu(h, wg_ref, wu_ref, wd_ref)


def _ffn_call(x, g, wg, wu, wd, tm):
    n, d = x.shape
    dff = wg.shape[1]
    return pl.pallas_call(
        _ffn_kernel,
        grid=(n // tm,),
        in_specs=[
            pl.BlockSpec((tm, d), lambda i: (i, 0)),
            _const_spec((1, d)),
            _const_spec((d, dff)),
            _const_spec((d, dff)),
            _const_spec((dff, d)),
        ],
        out_specs=pl.BlockSpec((tm, d), lambda i: (i, 0)),
        out_shape=jax.ShapeDtypeStruct((n, d), F32),
        compiler_params=pltpu.CompilerParams(
            dimension_semantics=("arbitrary",), vmem_limit_bytes=VMEM_LIMIT),
        name="ffn",
    )(x, g, wg, wu, wd)


def _out_ffn_kernel(x_ref, c_ref, a_ref, woc_ref, woa_ref, g_ref, wg_ref, wu_ref, wd_ref, o_ref):
    x = x_ref[...] + (_dot(c_ref[...], woc_ref[...]) + _dot(a_ref[...], woa_ref[...]))
    h = _rmsnorm(x, g_ref[...]).astype(BF16)
    o_ref[...] = x + 0.5 * _swiglu(h, wg_ref, wu_ref, wd_ref)


def _out_ffn_call(x, conv_out, attn, woc, woa, g, wg, wu, wd, tm):
    n, d = x.shape
    dff = wg.shape[1]
    dc = conv_out.shape[1]
    da = attn.shape[1]
    return pl.pallas_call(
        _out_ffn_kernel,
        grid=(n // tm,),
        in_specs=[
            pl.BlockSpec((tm, d), lambda i: (i, 0)),
            pl.BlockSpec((tm, dc), lambda i: (i, 0)),
            pl.BlockSpec((tm, da), lambda i: (i, 0)),
            _const_spec((dc, d)),
            _const_spec((da, d)),
            _const_spec((1, d)),
            _const_spec((d, dff)),
            _const_spec((d, dff)),
            _const_spec((dff, d)),
        ],
        out_specs=pl.BlockSpec((tm, d), lambda i: (i, 0)),
        out_shape=jax.ShapeDtypeStruct((n, d), F32),
        compiler_params=pltpu.CompilerParams(
            dimension_semantics=("arbitrary",), vmem_limit_bytes=VMEM_LIMIT),
        name="out_ffn",
    )(x, conv_out, attn, woc, woa, g, wg, wu, wd)


def _rope(y, cos, sin_lo, sin_hi):
    half = ROT_DIM // 2
    return (y * cos + pltpu.roll(y, LANES - half, 1) * sin_lo) + pltpu.roll(y, half, 1) * sin_hi


def _group_sumsq(x, bd):
    hi, lo = _split_bf16(x * x)
    return _dot(hi.astype(BF16), bd) + _dot(lo.astype(BF16), bd)


def _head_norm(x, g, bd):
    ms = _group_sumsq(x, bd) * (1.0 / HEAD_DIM)
    return (x * lax.rsqrt(ms + EPS)) * g


def _stack_keys(kin):
    lane = lax.broadcasted_iota(I32, kin.shape, 1)
    first = lane < IDX_DIM
    hi, lo = _split_bf16(kin)
    c0 = jnp.where(first, hi, pltpu.roll(hi, IDX_DIM, 1))
    c1 = jnp.where(first, lo, 0.0)
    return jnp.concatenate([c0, c1], axis=1).astype(BF16)


def _stack_queries(qc):
    lane = lax.broadcasted_iota(I32, qc.shape, 1)
    first = lane < IDX_DIM
    hi, lo = _split_bf16(qc)
    hi_sw = pltpu.roll(hi, IDX_DIM, 1)
    lo_sw = pltpu.roll(lo, IDX_DIM, 1)
    a0 = jnp.where(first, hi, lo_sw)
    a1 = jnp.where(first, hi, 0.0)
    b0 = jnp.where(first, hi_sw, lo)
    b1 = jnp.where(first, hi_sw, 0.0)
    return jnp.concatenate([a0, a1, b0, b1], axis=1).astype(BF16)


def _proj_kernel(x_ref, gmix_ref, wmain_ref, wih_ref, wil_ref, convw_ref, cprev_ref,
                 gq_ref, gk_ref, gik_ref, cos_ref, slo_ref, shi_ref, bd_ref,
                 convout_ref, q_ref, k_ref, v_ref, kb_ref, qi3_ref, kiwi_ref, ki3_ref, ulast_ref,
                 ubuf_ref, *, tiles_per_batch, d_conv):
    i = pl.program_id(0)
    tm = x_ref.shape[0]

    @pl.when(i % tiles_per_batch == 0)
    def _():
        ubuf_ref[0:SUBLANES, :] = cprev_ref[0]

    h = _rmsnorm(x_ref[...], gmix_ref[...])
    h_hi = h.astype(BF16)
    h_lo = (h - h_hi.astype(F32)).astype(BF16)
    q0 = 3 * d_conv
    nq = N_HEADS * HEAD_DIM
    n_main = wmain_ref.shape[1]
    zq = _dot(h_hi, wmain_ref[:, q0:n_main])
    wih = wih_ref[...]
    zi = (_dot(h_hi, wih) + _dot(h_lo, wih)) + _dot(h_hi, wil_ref[...])

    cos = cos_ref[...]
    slo = slo_ref[...]
    shi = shi_ref[...]
    bd = bd_ref[...]
    scale = HEAD_DIM ** -0.5 * LOG2E
    for c in range(nq // LANES):
        qc = zq[:, c * LANES:(c + 1) * LANES]
        qn = _head_norm(qc, gq_ref[...], bd)
        q_ref[:, c * LANES:(c + 1) * LANES] = (_rope(qn, cos, slo, shi) * scale).astype(BF16)
    kn = _rope(_head_norm(zq[:, nq:nq + LANES], gk_ref[...], bd), cos, slo, shi)
    k_ref[...] = kn
    kb_ref[...] = kn.astype(BF16)
    v_ref[...] = zq[:, nq + LANES:nq + 2 * LANES]

    zc = _dot(h_hi, wmain_ref[:, 0:q0])

    ni = N_IDX_HEADS * IDX_DIM
    for c in range(ni // LANES):
        qic = _rope(zi[:, c * LANES:(c + 1) * LANES], cos, slo, shi)
        qi3_ref[:, c * 4 * LANES:(c + 1) * 4 * LANES] = _stack_queries(qic)
    kw = zi[:, ni:ni + LANES]
    kin = _rope(_head_norm(kw, gik_ref[...], bd), cos, slo, shi)
    lane = lax.broadcasted_iota(I32, kw.shape, 1)
    kiwi_ref[...] = jnp.where(lane < IDX_DIM, kin, kw * (N_IDX_HEADS ** -0.5))
    ki3_ref[...] = _stack_keys(kin)

    gb = zc[:, 0:d_conv]
    u = zc[:, d_conv:2 * d_conv] * zc[:, 2 * d_conv:3 * d_conv]
    ubuf_ref[SUBLANES:SUBLANES + tm, :] = u
    w = convw_ref[...]
    conv = (ubuf_ref[SUBLANES - 2:SUBLANES - 2 + tm, :] * w[0:1, :]
            + ubuf_ref[SUBLANES - 1:SUBLANES - 1 + tm, :] * w[1:2, :]) + u * w[2:3, :]
    convout_ref[...] = (gb * conv).astype(BF16)
    tail = u[tm - SUBLANES:tm, :]
    ulast_ref[0] = tail
    ubuf_ref[0:SUBLANES, :] = tail


def _proj_call(x1, cprev, wts, tables, tm, tiles_per_batch):
    n, d = x1.shape
    (gmix, wmain, wih, wil, convw, gq, gk, gik, bd) = wts
    cos, slo, shi = tables
    d_conv = convw.shape[1]
    nt = n // tm
    row = lambda w: pl.BlockSpec((tm, w), lambda i: (i, 0))
    tab = pl.BlockSpec((tm, LANES), lambda i: (i % tiles_per_batch, 0))
    nq = N_HEADS * HEAD_DIM
    out_shape = (
        jax.ShapeDtypeStruct((n, d_conv), BF16),
        jax.ShapeDtypeStruct((n, nq), BF16),
        jax.ShapeDtypeStruct((n, LANES), F32),
        jax.ShapeDtypeStruct((n, LANES), F32),
        jax.ShapeDtypeStruct((n, LANES), BF16),
        jax.ShapeDtypeStruct((n, N_IDX_HEADS * 2 * LANES), BF16),
        jax.ShapeDtypeStruct((n, LANES), F32),
        jax.ShapeDtypeStruct((n, 2 * LANES), BF16),
        jax.ShapeDtypeStruct((nt, SUBLANES, d_conv), F32),
    )
    out_specs = (
        row(d_conv), row(nq), row(LANES), row(LANES), row(LANES), row(N_IDX_HEADS * 2 * LANES),
        row(LANES), row(2 * LANES),
        pl.BlockSpec((1, SUBLANES, d_conv), lambda i: (i, 0, 0)),
    )
    in_specs = [
        row(d),
        _const_spec(gmix.shape), _const_spec(wmain.shape), _const_spec(wih.shape), _const_spec(wil.shape),
        _const_spec(convw.shape),
        pl.BlockSpec((1, SUBLANES, d_conv), lambda i: (i // tiles_per_batch, 0, 0)),
        _const_spec(gq.shape), _const_spec(gk.shape), _const_spec(gik.shape),
        tab, tab, tab,
        _const_spec(bd.shape),
    ]
    return pl.pallas_call(
        functools.partial(_proj_kernel, tiles_per_batch=tiles_per_batch, d_conv=d_conv),
        grid=(nt,),
        in_specs=in_specs,
        out_specs=out_specs,
        out_shape=out_shape,
        scratch_shapes=[pltpu.VMEM((tm + SUBLANES, d_conv), F32)],
        compiler_params=pltpu.CompilerParams(
            dimension_semantics=("arbitrary",), vmem_limit_bytes=VMEM_LIMIT),
        name="proj",
    )(x1, gmix, wmain, wih, wil, convw, cprev, gq, gk, gik, cos, slo, shi, bd)


def _stack_keys_kernel(k_ref, o_ref):
    o_ref[...] = _stack_keys(k_ref[...])


def _stack_keys_call(kpad, tm):
    n = kpad.shape[0]
    return pl.pallas_call(
        _stack_keys_kernel,
        grid=(n // tm,),
        in_specs=[pl.BlockSpec((tm, LANES), lambda i: (i, 0))],
        out_specs=pl.BlockSpec((tm, 2 * LANES), lambda i: (i, 0)),
        out_shape=jax.ShapeDtypeStruct((n, 2 * LANES), BF16),
        name="stack_keys",
    )(kpad)


def _attn_kernel(qi3_ref, wit_ref, q_ref, ki3_ref, kb_ref, vt_ref, o_ref,
                 score_ref, sb_ref, bias_ref, ot_ref, m_ref, l_ref, s_ref, *, kc, past, n_keys, n_sel, idx_bits):
    tq = q_ref.shape[0]
    j = pl.program_id(1)
    q_first = past + j * tq
    pos = q_first + lax.broadcasted_iota(I32, (1, tq), 1)
    limit = jnp.minimum(((pos >> 6) + 1) * CHUNK, n_keys)
    max_limit = jnp.minimum((((q_first + tq - 1) >> 6) + 1) * CHUNK, n_keys)
    nk = (max_limit + kc - 1) // kc

    def rows(c):
        return pl.ds(pl.multiple_of(c * kc, kc), kc)

    def key_pos(c):
        return c * kc + lax.broadcasted_iota(I32, (kc, tq), 0)

    w8 = wit_ref[0] * (IDX_DIM ** -0.5)

    def score_body(c, carry):
        kk = ki3_ref[rows(c), :]
        for hh in range(N_IDX_HEADS):
            d = _dot_nt(kk, qi3_ref[:, hh * 2 * LANES:(hh + 1) * 2 * LANES])
            term = jnp.maximum(d, 0.0) * w8[hh:hh + 1, :]
            score_ref[rows(c), :] = term if hh == 0 else score_ref[rows(c), :] + term
        score = jnp.where(key_pos(c) < limit, score_ref[rows(c), :] + 0.0, -jnp.inf)
        score_ref[rows(c), :] = score
        sb_ref[rows(c), :] = score.astype(BF16)
        return carry

    lax.fori_loop(0, nk, score_body, 0)

    def count(pred):
        def body(c, acc):
            m = pred(c, score_ref[rows(c), :]).astype(I32)
            return acc + m.reshape(kc // SUBLANES, SUBLANES, tq).sum(axis=0)
        acc = lax.fori_loop(0, nk, body, jnp.zeros((SUBLANES, tq), I32))
        return acc.sum(axis=0, keepdims=True)

    def as_float(t):
        return pltpu.bitcast(t ^ ((t >> 31) & 0x7FFFFFFF), F32)

    def as_bf16_tile(t):
        bits = t ^ ((t >> 15) & 0x7FFF)
        return pltpu.bitcast(jnp.broadcast_to(bits, (PACKED_ROWS, tq)).astype(I16), BF16)

    def count_coarse(t):
        cand = as_bf16_tile(t)
        one = jnp.ones((PACKED_ROWS, tq), I16)
        zero = jnp.zeros((PACKED_ROWS, tq), I16)

        def body(c, acc):
            s = sb_ref[rows(c), :]
            for r in range(kc // PACKED_ROWS):
                hit = s[r * PACKED_ROWS:(r + 1) * PACKED_ROWS, :] >= cand
                acc = acc + jnp.where(hit, one, zero)
            return acc

        acc = lax.fori_loop(0, nk, body, zero)
        return acc.astype(I32).sum(axis=0, keepdims=True)

    c_nonneg = count_coarse(jnp.zeros((1, tq), I32))
    t1 = jnp.where(c_nonneg >= n_sel, 0, I16_MIN).astype(I32)

    def coarse_body(b, t):
        cand = t | (jnp.int32(1) << (14 - b))
        return jnp.where(count_coarse(cand) >= n_sel, cand, t)

    t1 = lax.fori_loop(0, 15, coarse_body, t1)
    live = t1 != I16_MIN
    image1 = t1 * 65536 + jnp.where(t1 < 0, 0xFFFF, 0)
    base = image1 - 0x8001

    def fine_body(b, t):
        cand = t + (jnp.int32(1) << (16 - b))
        cand_f = as_float(cand)
        cnt = count(lambda c, s: s >= cand_f)
        return jnp.where(cnt >= n_sel, cand, t)

    t_sel = lax.fori_loop(0, 17, fine_body, base)
    thr = jnp.where(live, as_float(t_sel), -jnp.inf)

    c_gt = count(lambda c, s: s > thr)
    c_ge = count(lambda c, s: s >= thr)
    room = n_sel - c_gt
    surplus = jnp.max(jnp.where(live & (c_ge > n_sel), 1, 0))

    def tie_search():
        def body(b, jm):
            cand = jm | (jnp.int32(1) << (idx_bits - 1 - b))
            cnt = count(lambda c, s: (s == thr) & (key_pos(c) < cand))
            return jnp.where(cnt < room, cand, jm)
        return lax.fori_loop(0, idx_bits, body, jnp.zeros((1, tq), I32))

    jm = lax.cond(surplus > 0, tie_search, lambda: jnp.full((1, tq), 2 ** idx_bits, I32))
    jm = jnp.where(live, jm, -1)

    def bias_body(c, carry):
        s = score_ref[rows(c), :]
        sel = (s > thr) | ((s == thr) & (key_pos(c) <= jm))
        bias_ref[rows(c), :] = jnp.where(sel, 0.0, NEG_BIG)
        return carry

    lax.fori_loop(0, nk, bias_body, 0)

    group = N_HEADS // N_KV_HEADS
    m_ref[...] = jnp.full(m_ref.shape, NEG_BIG, F32)
    l_ref[...] = jnp.zeros(l_ref.shape, F32)
    ot_ref[...] = jnp.zeros(ot_ref.shape, F32)

    def fold(x, op):
        return op(x.reshape(kc // SUBLANES, SUBLANES, tq), axis=0)

    def logit_body(c, carry):
        for hh in range(N_HEADS):
            kv = hh // group
            part = slice(hh * SUBLANES, (hh + 1) * SUBLANES)
            kk = kb_ref[rows(c), kv * HEAD_DIM:(kv + 1) * HEAD_DIM]
            s = _dot_nt(kk, q_ref[:, hh * HEAD_DIM:(hh + 1) * HEAD_DIM]) + bias_ref[rows(c), :]
            s_ref[hh, rows(c), :] = s
            m_ref[part, :] = jnp.maximum(m_ref[part, :], fold(s, jnp.max))
        return carry

    lax.fori_loop(0, nk, logit_body, 0)

    def pv_body(c, carry):
        for hh in range(N_HEADS):
            kv = hh // group
            part = slice(hh * SUBLANES, (hh + 1) * SUBLANES)
            head = slice(hh * HEAD_DIM, (hh + 1) * HEAD_DIM)
            m = m_ref[part, :].max(axis=0, keepdims=True)
            p = jnp.exp2(s_ref[hh, rows(c), :] - m)
            l_ref[part, :] = l_ref[part, :] + fold(p, jnp.sum)
            vt = vt_ref[c, kv * HEAD_DIM:(kv + 1) * HEAD_DIM, :]
            ot_ref[head, :] = ot_ref[head, :] + _dot(vt, p.astype(BF16))
        return carry

    lax.fori_loop(0, nk, pv_body, 0)
    for hh in range(N_HEADS):
        part = slice(hh * SUBLANES, (hh + 1) * SUBLANES)
        head = slice(hh * HEAD_DIM, (hh + 1) * HEAD_DIM)
        ot_ref[head, :] = ot_ref[head, :] / l_ref[part, :].sum(axis=0, keepdims=True)
    o_ref[...] = ot_ref[...].T.astype(BF16)


def _attn_call(qi3, wit, q, ki3, kb, vt, *, batch, tq, kc, past, n_keys, n_sel):
    tq_total = q.shape[0] // batch
    nq = tq_total // tq
    lp = ki3.shape[1]
    idx_bits = max(1, int(np.ceil(np.log2(lp))))
    d_attn = q.shape[1]
    kernel = functools.partial(_attn_kernel, kc=kc, past=past, n_keys=n_keys, n_sel=n_sel, idx_bits=idx_bits)
    return pl.pallas_call(
        kernel,
        grid=(batch, nq),
        in_specs=[
            pl.BlockSpec((tq, qi3.shape[1]), lambda b, j: (b * nq + j, 0)),
            pl.BlockSpec((1, N_IDX_HEADS, tq), lambda b, j: (b, 0, j)),
            pl.BlockSpec((tq, d_attn), lambda b, j: (b * nq + j, 0)),
            pl.BlockSpec((None, lp, ki3.shape[2]), lambda b, j: (b, 0, 0)),
            pl.BlockSpec((None, lp, kb.shape[2]), lambda b, j: (b, 0, 0)),
            pl.BlockSpec((None, lp // kc, vt.shape[2], kc), lambda b, j: (b, 0, 0, 0)),
        ],
        out_specs=pl.BlockSpec((tq, d_attn), lambda b, j: (b * nq + j, 0)),
        out_shape=jax.ShapeDtypeStruct((batch * tq_total, d_attn), BF16),
        scratch_shapes=[
            pltpu.VMEM((lp, tq), F32),
            pltpu.VMEM((lp, tq), BF16),
            pltpu.VMEM((lp, tq), F32),
            pltpu.VMEM((d_attn, tq), F32),
            pltpu.VMEM((N_HEADS * SUBLANES, tq), F32),
            pltpu.VMEM((N_HEADS * SUBLANES, tq), F32),
            pltpu.VMEM((N_HEADS, lp, tq), F32),
        ],
        compiler_params=pltpu.CompilerParams(
            dimension_semantics=("arbitrary", "arbitrary"), vmem_limit_bytes=VMEM_LIMIT),
        name="attn",
    )(qi3, wit, q, ki3, kb, vt)


def _rope_tables(pos):
    half = ROT_DIM // 2
    inv = ROPE_THETA ** (-np.arange(half, dtype=np.float64) * (2.0 / ROT_DIM))
    ang = np.asarray(pos, np.float64)[:, None] * inv[None, :]
    cos = np.ones((len(pos), HEAD_DIM))
    slo = np.zeros((len(pos), HEAD_DIM))
    shi = np.zeros((len(pos), HEAD_DIM))
    cos[:, :half] = np.cos(ang)
    cos[:, half:ROT_DIM] = np.cos(ang)
    slo[:, :half] = -np.sin(ang)
    shi[:, half:ROT_DIM] = np.sin(ang)
    rep = LANES // HEAD_DIM
    return tuple(jnp.asarray(np.tile(t, (1, rep)), F32) for t in (cos, slo, shi))


def _pad_rows(a, rows):
    return jnp.pad(a, ((0, 0), (0, rows - a.shape[1])) + ((0, 0),) * (a.ndim - 2))


def _layer(x, past, cache, wts, *, tm_ffn, tm_proj, tq, kc):
    (ffn1, projw, woc, woa, ffn2) = wts
    b, t, d = x.shape
    n = b * t
    cache_k, cache_v, cache_ik, conv_prev = cache
    x1 = _ffn_call(x.reshape(n, d), *ffn1, tm=tm_ffn)

    d_conv = conv_prev.shape[-1]
    cprev = jnp.pad(conv_prev, ((0, 0), (SUBLANES - (CONV_W - 1), 0), (0, 0)))
    tables = _rope_tables(past + np.arange(t))
    conv_out, q, k, v, kb, qi3, kiwi, ki3, ulast = _proj_call(
        x1, cprev, projw, tables, tm_proj, t // tm_proj)

    n_keys = past + t
    lp = -(-n_keys // kc) * kc
    kb_all = kb.reshape(b, t, LANES)
    v_all = v.reshape(b, t, LANES).astype(BF16)
    ki3_all = ki3.reshape(b, t, 2 * LANES)
    if past:
        ck = cache_k.reshape(b, past, LANES).astype(BF16)
        cv = cache_v.reshape(b, past, LANES).astype(BF16)
        cik = jnp.pad(cache_ik, ((0, 0), (0, 0), (0, LANES - IDX_DIM))).reshape(b * past, LANES)
        cik3 = _stack_keys_call(cik, past).reshape(b, past, 2 * LANES)
        kb_all = jnp.concatenate([ck, kb_all], axis=1)
        v_all = jnp.concatenate([cv, v_all], axis=1)
        ki3_all = jnp.concatenate([cik3, ki3_all], axis=1)
    kb_all = _pad_rows(kb_all, lp)
    ki3_all = _pad_rows(ki3_all, lp)
    vt = jnp.swapaxes(_pad_rows(v_all, lp).reshape(b, lp // kc, kc, LANES), 2, 3)

    tqp = -(-t // tq) * tq
    wi = kiwi[:, IDX_DIM:IDX_DIM + N_IDX_HEADS].reshape(b, t, N_IDX_HEADS)
    wit = jnp.swapaxes(_pad_rows(wi, tqp), 1, 2)
    if tqp == t:
        padq = lambda a: a
    else:
        padq = lambda a: _pad_rows(a.reshape(b, t, a.shape[-1]), tqp).reshape(b * tqp, a.shape[-1])
    n_sel = max(1, min(TOP_K, n_keys // 4))
    attn = _attn_call(padq(qi3), wit, padq(q), ki3_all, kb_all, vt,
                      batch=b, tq=tq, kc=kc, past=past, n_keys=n_keys, n_sel=n_sel)
    if tqp != t:
        attn = attn.reshape(b, tqp, -1)[:, :t].reshape(n, -1)

    y = _out_ffn_call(x1, conv_out, attn, woc, woa, *ffn2, tm=tm_ffn)

    tiles = t // tm_proj
    conv_state = ulast.reshape(b, tiles, SUBLANES, d_conv)[:, -1, SUBLANES - (CONV_W - 1):]
    return (y.reshape(b, t, d),
            k.reshape(b, t, N_KV_HEADS, HEAD_DIM),
            v.reshape(b, t, N_KV_HEADS, HEAD_DIM),
            kiwi[:, :IDX_DIM].reshape(b, t, IDX_DIM),
            conv_state)


def _prep_weights(l, ffn1_norm, ffn1_w_gate, ffn1_w_up, ffn1_w_down, mix_norm, w_in, conv_w,
                  q_norm, k_norm, idx_k_norm, w_out, ffn2_norm, ffn2_w_gate, ffn2_w_up, ffn2_w_down):
    d_conv = conv_w.shape[-1]
    n_main = 3 * d_conv + N_HEADS * HEAD_DIM + 2 * N_KV_HEADS * HEAD_DIM
    row = lambda g: g[l][None, :]
    ffn1 = (row(ffn1_norm), ffn1_w_gate[l].astype(BF16), ffn1_w_up[l].astype(BF16), ffn1_w_down[l].astype(BF16))
    ffn2 = (row(ffn2_norm), ffn2_w_gate[l].astype(BF16), ffn2_w_up[l].astype(BF16), ffn2_w_down[l].astype(BF16))
    w = w_in[l]
    wmain = w[:, :n_main].astype(BF16)
    widx = w[:, n_main:]
    widx = jnp.pad(widx, ((0, 0), (0, -widx.shape[1] % LANES)))
    wih = widx.astype(BF16)
    wil = (widx - wih.astype(F32)).astype(BF16)
    gq = jnp.tile(q_norm[l], LANES // HEAD_DIM)[None, :]
    gk = jnp.tile(k_norm[l], LANES // HEAD_DIM)[None, :]
    gik = jnp.pad(idx_k_norm[l], (0, LANES - IDX_DIM))[None, :]
    lane = np.arange(LANES)
    bd = jnp.asarray(lane[:, None] // HEAD_DIM == lane[None, :] // HEAD_DIM, BF16)
    projw = (row(mix_norm), wmain, wih, wil, conv_w[l], gq, gk, gik, bd)
    wo = w_out[l].astype(BF16)
    return (ffn1, projw, wo[:d_conv], wo[d_conv:], ffn2)


def kernel(x_prompt, x_sample, cache_k, cache_v, cache_idx_k, state_conv, ffn1_norm, ffn1_w_gate, ffn1_w_up, ffn1_w_down, mix_norm, w_in, conv_w, q_norm, k_norm, idx_k_norm, w_out, ffn2_norm, ffn2_w_gate, ffn2_w_up, ffn2_w_down):
    depth = w_in.shape[0]
    bp, tp, _ = x_prompt.shape
    bs, ts, _ = x_sample.shape
    past = cache_k.shape[2]
    d_conv = conv_w.shape[-1]
    hp, hs = x_prompt, x_sample
    outs_p, outs_s = [], []
    for l in range(depth):
        wts = _prep_weights(l, ffn1_norm, ffn1_w_gate, ffn1_w_up, ffn1_w_down, mix_norm, w_in, conv_w,
                            q_norm, k_norm, idx_k_norm, w_out, ffn2_norm, ffn2_w_gate, ffn2_w_up, ffn2_w_down)
        empty = (None, None, None, jnp.zeros((bp, CONV_W - 1, d_conv), F32))
        hp, *rest_p = _layer(hp, 0, empty, wts, tm_ffn=512, tm_proj=256, tq=256, kc=256)
        cache = (cache_k[l], cache_v[l], cache_idx_k[l], state_conv[l])
        hs, *rest_s = _layer(hs, past, cache, wts, tm_ffn=bs * ts, tm_proj=ts, tq=128, kc=256)
        outs_p.append(rest_p)
        outs_s.append(rest_s)
    stack = lambda outs, i: jnp.stack([o[i] for o in outs])
    return (hp, hs,
            stack(outs_p, 0), stack(outs_p, 1), stack(outs_p, 2), stack(outs_p, 3),
            stack(outs_s, 0), stack(outs_s, 1), stack(outs_s, 2), stack(outs_s, 3))
```

```python
import functools

import numpy as np
import jax
import jax.numpy as jnp
from jax import lax
from jax.experimental import pallas as pl
from jax.experimental.pallas import tpu as pltpu

F32 = jnp.float32
BF16 = jnp.bfloat16
I32 = jnp.int32
I16 = jnp.int16

CHUNK = 64
CONV_W = 3
N_HEADS = 8
N_KV_HEADS = 2
HEAD_DIM = 64
ROT_DIM = 16
ROPE_THETA = 500000.0
N_IDX_HEADS = 8
IDX_DIM = 64
TOP_K = 256
EPS = 1e-6

LANES = 128
SUBLANES = 8
PACKED_ROWS = 16
INT_MIN = -(2 ** 31)
I16_MIN = -(2 ** 15)
LOG2E = 1.4426950408889634
NEG_BIG = -1e30
VMEM_LIMIT = 56 * 1024 * 1024


def _const_spec(shape):
    nd = len(shape)
    return pl.BlockSpec(shape, lambda *_: (0,) * nd, pipeline_mode=pl.Buffered(1))


def _dot(a, b):
    return jnp.dot(a, b, preferred_element_type=F32)


def _dot_nt(a, b):
    return lax.dot_general(a, b, (((1,), (1,)), ((), ())), preferred_element_type=F32)


def _split_bf16(x):
    hi = x.astype(BF16).astype(F32)
    lo = (x - hi).astype(BF16).astype(F32)
    return hi, lo


def _rmsnorm(x, g):
    ms = jnp.mean(x * x, axis=-1, keepdims=True)
    return (x * lax.rsqrt(ms + EPS)) * g


def _swiglu(h, wg_ref, wu_ref, wd_ref):
    g = _dot(h, wg_ref[...])
    u = _dot(h, wu_ref[...])
    a = (g * jax.nn.sigmoid(g)) * u
    return _dot(a.astype(BF16), wd_ref[...])


def _ffn_kernel(x_ref, g_ref, wg_ref, wu_ref, wd_ref, o_ref):
    x = x_ref[...]
    h = _rmsnorm(x, g_ref[...]).astype(BF16)
    o_ref[...] = x + 0.5 * _swiglu(h, wg_ref, wu_ref, wd_ref)


def _ffn_call(x, g, wg, wu, wd, tm):
    n, d = x.shape
    dff = wg.shape[1]
    return pl.pallas_call(
        _ffn_kernel,
        grid=(n // tm,),
        in_specs=[
            pl.BlockSpec((tm, d), lambda i: (i, 0)),
            _const_spec((1, d)),
            _const_spec((d, dff)),
            _const_spec((d, dff)),
            _const_spec((dff, d)),
        ],
        out_specs=pl.BlockSpec((tm, d), lambda i: (i, 0)),
        out_shape=jax.ShapeDtypeStruct((n, d), F32),
        compiler_params=pltpu.CompilerParams(
            dimension_semantics=("arbitrary",), vmem_limit_bytes=VMEM_LIMIT),
        name="ffn",
    )(x, g, wg, wu, wd)


def _out_ffn_kernel(x_ref, c_ref, a_ref, woc_ref, woa_ref, g_ref, wg_ref, wu_ref, wd_ref, o_ref):
    x = x_ref[...] + (_dot(c_ref[...], woc_ref[...]) + _dot(a_ref[...], woa_ref[...]))
    h = _rmsnorm(x, g_ref[...]).astype(BF16)
    o_ref[...] = x + 0.5 * _swiglu(h, wg_ref, wu_ref, wd_ref)


def _out_ffn_call(x, conv_out, attn, woc, woa, g, wg, wu, wd, tm):
    n, d = x.shape
    dff = wg.shape[1]
    dc = conv_out.shape[1]
    da = attn.shape[1]
    return pl.pallas_call(
        _out_ffn_kernel,
        grid=(n // tm,),
        in_specs=[
            pl.BlockSpec((tm, d), lambda i: (i, 0)),
            pl.BlockSpec((tm, dc), lambda i: (i, 0)),
            pl.BlockSpec((tm, da), lambda i: (i, 0)),
            _const_spec((dc, d)),
            _const_spec((da, d)),
            _const_spec((1, d)),
            _const_spec((d, dff)),
            _const_spec((d, dff)),
            _const_spec((dff, d)),
        ],
        out_specs=pl.BlockSpec((tm, d), lambda i: (i, 0)),
        out_shape=jax.ShapeDtypeStruct((n, d), F32),
        compiler_params=pltpu.CompilerParams(
            dimension_semantics=("arbitrary",), vmem_limit_bytes=VMEM_LIMIT),
        name="out_ffn",
    )(x, conv_out, attn, woc, woa, g, wg, wu, wd)


def _rope(y, cos, sin_lo, sin_hi):
    half = ROT_DIM // 2
    return (y * cos + pltpu.roll(y, LANES - half, 1) * sin_lo) + pltpu.roll(y, half, 1) * sin_hi


def _group_sumsq(x, bd):
    hi, lo = _split_bf16(x * x)
    return _dot(hi.astype(BF16), bd) + _dot(lo.astype(BF16), bd)


def _head_norm(x, g, bd):
    ms = _group_sumsq(x, bd) * (1.0 / HEAD_DIM)
    return (x * lax.rsqrt(ms + EPS)) * g


def _stack_keys(kin):
    lane = lax.broadcasted_iota(I32, kin.shape, 1)
    first = lane < IDX_DIM
    hi, lo = _split_bf16(kin)
    c0 = jnp.where(first, hi, pltpu.roll(hi, IDX_DIM, 1))
    c1 = jnp.where(first, lo, 0.0)
    return jnp.concatenate([c0, c1], axis=1).astype(BF16)


def _stack_queries(qc):
    lane = lax.broadcasted_iota(I32, qc.shape, 1)
    first = lane < IDX_DIM
    hi, lo = _split_bf16(qc)
    hi_sw = pltpu.roll(hi, IDX_DIM, 1)
    lo_sw = pltpu.roll(lo, IDX_DIM, 1)
    a0 = jnp.where(first, hi, lo_sw)
    a1 = jnp.where(first, hi, 0.0)
    b0 = jnp.where(first, hi_sw, lo)
    b1 = jnp.where(first, hi_sw, 0.0)
    return jnp.concatenate([a0, a1, b0, b1], axis=1).astype(BF16)


def _proj_kernel(x_ref, gmix_ref, wmain_ref, wih_ref, wil_ref, convw_ref, cprev_ref,
                 gq_ref, gk_ref, gik_ref, cos_ref, slo_ref, shi_ref, bd_ref,
                 convout_ref, q_ref, k_ref, v_ref, kb_ref, qi3_ref, kiwi_ref, ki3_ref, ulast_ref,
                 ubuf_ref, *, tiles_per_batch, d_conv):
    i = pl.program_id(0)
    tm = x_ref.shape[0]

    @pl.when(i % tiles_per_batch == 0)
    def _():
        ubuf_ref[0:SUBLANES, :] = cprev_ref[0]

    h = _rmsnorm(x_ref[...], gmix_ref[...])
    h_hi = h.astype(BF16)
    h_lo = (h - h_hi.astype(F32)).astype(BF16)
    q0 = 3 * d_conv
    nq = N_HEADS * HEAD_DIM
    n_main = wmain_ref.shape[1]
    zq = _dot(h_hi, wmain_ref[:, q0:n_main])
    wih = wih_ref[...]
    zi = (_dot(h_hi, wih) + _dot(h_lo, wih)) + _dot(h_hi, wil_ref[...])

    cos = cos_ref[...]
    slo = slo_ref[...]
    shi = shi_ref[...]
    bd = bd_ref[...]
    scale = HEAD_DIM ** -0.5 * LOG2E
    for c in range(nq // LANES):
        qc = zq[:, c * LANES:(c + 1) * LANES]
        qn = _head_norm(qc, gq_ref[...], bd)
        q_ref[:, c * LANES:(c + 1) * LANES] = (_rope(qn, cos, slo, shi) * scale).astype(BF16)
    kn = _rope(_head_norm(zq[:, nq:nq + LANES], gk_ref[...], bd), cos, slo, shi)
    k_ref[...] = kn
    kb_ref[...] = kn.astype(BF16)
    v_ref[...] = zq[:, nq + LANES:nq + 2 * LANES]

    zc = _dot(h_hi, wmain_ref[:, 0:q0])

    ni = N_IDX_HEADS * IDX_DIM
    for c in range(ni // LANES):
        qic = _rope(zi[:, c * LANES:(c + 1) * LANES], cos, slo, shi)
        qi3_ref[:, c * 4 * LANES:(c + 1) * 4 * LANES] = _stack_queries(qic)
    kw = zi[:, ni:ni + LANES]
    kin = _rope(_head_norm(kw, gik_ref[...], bd), cos, slo, shi)
    lane = lax.broadcasted_iota(I32, kw.shape, 1)
    kiwi_ref[...] = jnp.where(lane < IDX_DIM, kin, kw * (N_IDX_HEADS ** -0.5))
    ki3_ref[...] = _stack_keys(kin)

    gb = zc[:, 0:d_conv]
    u = zc[:, d_conv:2 * d_conv] * zc[:, 2 * d_conv:3 * d_conv]
    ubuf_ref[SUBLANES:SUBLANES + tm, :] = u
    w = convw_ref[...]
    conv = (ubuf_ref[SUBLANES - 2:SUBLANES - 2 + tm, :] * w[0:1, :]
            + ubuf_ref[SUBLANES - 1:SUBLANES - 1 + tm, :] * w[1:2, :]) + u * w[2:3, :]
    convout_ref[...] = (gb * conv).astype(BF16)
    tail = u[tm - SUBLANES:tm, :]
    ulast_ref[0] = tail
    ubuf_ref[0:SUBLANES, :] = tail


def _proj_call(x1, cprev, wts, tables, tm, tiles_per_batch):
    n, d = x1.shape
    (gmix, wmain, wih, wil, convw, gq, gk, gik, bd) = wts
    cos, slo, shi = tables
    d_conv = convw.shape[1]
    nt = n // tm
    row = lambda w: pl.BlockSpec((tm, w), lambda i: (i, 0))
    tab = pl.BlockSpec((tm, LANES), lambda i: (i % tiles_per_batch, 0))
    nq = N_HEADS * HEAD_DIM
    out_shape = (
        jax.ShapeDtypeStruct((n, d_conv), BF16),
        jax.ShapeDtypeStruct((n, nq), BF16),
        jax.ShapeDtypeStruct((n, LANES), F32),
        jax.ShapeDtypeStruct((n, LANES), F32),
        jax.ShapeDtypeStruct((n, LANES), BF16),
        jax.ShapeDtypeStruct((n, N_IDX_HEADS * 2 * LANES), BF16),
        jax.ShapeDtypeStruct((n, LANES), F32),
        jax.ShapeDtypeStruct((n, 2 * LANES), BF16),
        jax.ShapeDtypeStruct((nt, SUBLANES, d_conv), F32),
    )
    out_specs = (
        row(d_conv), row(nq), row(LANES), row(LANES), row(LANES), row(N_IDX_HEADS * 2 * LANES),
        row(LANES), row(2 * LANES),
        pl.BlockSpec((1, SUBLANES, d_conv), lambda i: (i, 0, 0)),
    )
    in_specs = [
        row(d),
        _const_spec(gmix.shape), _const_spec(wmain.shape), _const_spec(wih.shape), _const_spec(wil.shape),
        _const_spec(convw.shape),
        pl.BlockSpec((1, SUBLANES, d_conv), lambda i: (i // tiles_per_batch, 0, 0)),
        _const_spec(gq.shape), _const_spec(gk.shape), _const_spec(gik.shape),
        tab, tab, tab,
        _const_spec(bd.shape),
    ]
    return pl.pallas_call(
        functools.partial(_proj_kernel, tiles_per_batch=tiles_per_batch, d_conv=d_conv),
        grid=(nt,),
        in_specs=in_specs,
        out_specs=out_specs,
        out_shape=out_shape,
        scratch_shapes=[pltpu.VMEM((tm + SUBLANES, d_conv), F32)],
        compiler_params=pltpu.CompilerParams(
            dimension_semantics=("arbitrary",), vmem_limit_bytes=VMEM_LIMIT),
        name="proj",
    )(x1, gmix, wmain, wih, wil, convw, cprev, gq, gk, gik, cos, slo, shi, bd)


def _stack_keys_kernel(k_ref, o_ref):
    o_ref[...] = _stack_keys(k_ref[...])


def _stack_keys_call(kpad, tm):
    n = kpad.shape[0]
    return pl.pallas_call(
        _stack_keys_kernel,
        grid=(n // tm,),
        in_specs=[pl.BlockSpec((tm, LANES), lambda i: (i, 0))],
        out_specs=pl.BlockSpec((tm, 2 * LANES), lambda i: (i, 0)),
        out_shape=jax.ShapeDtypeStruct((n, 2 * LANES), BF16),
        name="stack_keys",
    )(kpad)


def _attn_kernel(qi3_ref, wit_ref, q_ref, ki3_ref, kb_ref, vt_ref, o_ref,
                 score_ref, sb_ref, bias_ref, ot_ref, m_ref, l_ref, s_ref, *, kc, past, n_keys, n_sel, idx_bits):
    tq = q_ref.shape[0]
    j = pl.program_id(1)
    q_first = past + j * tq
    pos = q_first + lax.broadcasted_iota(I32, (1, tq), 1)
    limit = jnp.minimum(((pos >> 6) + 1) * CHUNK, n_keys)
    max_limit = jnp.minimum((((q_first + tq - 1) >> 6) + 1) * CHUNK, n_keys)
    nk = (max_limit + kc - 1) // kc

    def for_chunk_pairs(body):
        lax.fori_loop(0, nk // 2, lambda i, carry: body(2 * i + 1, body(2 * i, carry)), 0)

        @pl.when(nk % 2 == 1)
        def _():
            body(nk - 1, 0)

    def rows(c):
        return pl.ds(pl.multiple_of(c * kc, kc), kc)

    def key_pos(c):
        return c * kc + lax.broadcasted_iota(I32, (kc, tq), 0)

    w8 = wit_ref[0] * (IDX_DIM ** -0.5)

    def score_body(c, carry):
        kk = ki3_ref[rows(c), :]
        for hh in range(N_IDX_HEADS):
            d = _dot_nt(kk, qi3_ref[:, hh * 2 * LANES:(hh + 1) * 2 * LANES])
            term = jnp.maximum(d, 0.0) * w8[hh:hh + 1, :]
            score_ref[rows(c), :] = term if hh == 0 else score_ref[rows(c), :] + term
        score = jnp.where(key_pos(c) < limit, score_ref[rows(c), :] + 0.0, -jnp.inf)
        score_ref[rows(c), :] = score
        sb_ref[rows(c), :] = score.astype(BF16)
        return carry

    for_chunk_pairs(score_body)

    def count(pred):
        def body(c, acc):
            m = pred(c, score_ref[rows(c), :]).astype(I32)
            return acc + m.reshape(kc // SUBLANES, SUBLANES, tq).sum(axis=0)
        acc = lax.fori_loop(0, nk, body, jnp.zeros((SUBLANES, tq), I32))
        return acc.sum(axis=0, keepdims=True)

    def as_float(t):
        return pltpu.bitcast(t ^ ((t >> 31) & 0x7FFFFFFF), F32)

    def as_bf16_tile(t):
        bits = t ^ ((t >> 15) & 0x7FFF)
        return pltpu.bitcast(jnp.broadcast_to(bits, (PACKED_ROWS, tq)).astype(I16), BF16)

    def count_coarse(t):
        cand = as_bf16_tile(t)
        one = jnp.ones((PACKED_ROWS, tq), I16)
        zero = jnp.zeros((PACKED_ROWS, tq), I16)

        def body(c, acc):
            s = sb_ref[rows(c), :]
            for r in range(kc // PACKED_ROWS):
                hit = s[r * PACKED_ROWS:(r + 1) * PACKED_ROWS, :] >= cand
                acc = acc + jnp.where(hit, one, zero)
            return acc

        acc = lax.fori_loop(0, nk, body, zero)
        return acc.astype(I32).sum(axis=0, keepdims=True)

    c_nonneg = count_coarse(jnp.zeros((1, tq), I32))
    t1 = jnp.where(c_nonneg >= n_sel, 0, I16_MIN).astype(I32)

    def coarse_body(b, t):
        cand = t | (jnp.int32(1) << (14 - b))
        return jnp.where(count_coarse(cand) >= n_sel, cand, t)

    t1 = lax.fori_loop(0, 15, coarse_body, t1)
    live = t1 != I16_MIN
    image1 = t1 * 65536 + jnp.where(t1 < 0, 0xFFFF, 0)
    base = image1 - 0x8001

    def fine_body(b, carry):
        t, c_at = carry
        cand = t + (jnp.int32(1) << (16 - b))
        cand_f = as_float(cand)
        cnt = count(lambda c, s: s >= cand_f)
        ok = cnt >= n_sel
        return jnp.where(ok, cand, t), jnp.where(ok, cnt, c_at)

    t_sel, c_ge = lax.fori_loop(0, 17, fine_body, (base, jnp.full((1, tq), n_sel, I32)))
    thr = jnp.where(live, as_float(t_sel), -jnp.inf)

    surplus = jnp.max(jnp.where(live & (c_ge > n_sel), 1, 0))

    def tie_search():
        room = n_sel - count(lambda c, s: s > thr)

        def body(b, jm):
            cand = jm | (jnp.int32(1) << (idx_bits - 1 - b))
            cnt = count(lambda c, s: (s == thr) & (key_pos(c) < cand))
            return jnp.where(cnt < room, cand, jm)
        return lax.fori_loop(0, idx_bits, body, jnp.zeros((1, tq), I32))

    jm = lax.cond(surplus > 0, tie_search, lambda: jnp.full((1, tq), 2 ** idx_bits, I32))
    jm = jnp.where(live, jm, -1)

    def bias_body(c, carry):
        s = score_ref[rows(c), :]
        sel = (s > thr) | ((s == thr) & (key_pos(c) <= jm))
        bias_ref[rows(c), :] = jnp.where(sel, 0.0, NEG_BIG)
        return carry

    lax.fori_loop(0, nk, bias_body, 0)

    group = N_HEADS // N_KV_HEADS
    m_ref[...] = jnp.full(m_ref.shape, NEG_BIG, F32)
    l_ref[...] = jnp.zeros(l_ref.shape, F32)
    ot_ref[...] = jnp.zeros(ot_ref.shape, F32)

    def fold(x, op):
        return op(x.reshape(kc // SUBLANES, SUBLANES, tq), axis=0)

    def logit_body(c, carry):
        for hh in range(N_HEADS):
            kv = hh // group
            part = slice(hh * SUBLANES, (hh + 1) * SUBLANES)
            kk = kb_ref[rows(c), kv * HEAD_DIM:(kv + 1) * HEAD_DIM]
            s = _dot_nt(kk, q_ref[:, hh * HEAD_DIM:(hh + 1) * HEAD_DIM]) + bias_ref[rows(c), :]
            s_ref[hh, rows(c), :] = s
            m_ref[part, :] = jnp.maximum(m_ref[part, :], fold(s, jnp.max))
        return carry

    for_chunk_pairs(logit_body)

    def pv_body(c, carry):
        for hh in range(N_HEADS):
            kv = hh // group
            part = slice(hh * SUBLANES, (hh + 1) * SUBLANES)
            head = slice(hh * HEAD_DIM, (hh + 1) * HEAD_DIM)
            m = m_ref[part, :].max(axis=0, keepdims=True)
            p = jnp.exp2(s_ref[hh, rows(c), :] - m)
            l_ref[part, :] = l_ref[part, :] + fold(p, jnp.sum)
            vt = vt_ref[c, kv * HEAD_DIM:(kv + 1) * HEAD_DIM, :]
            ot_ref[head, :] = ot_ref[head, :] + _dot(vt, p.astype(BF16))
        return carry

    for_chunk_pairs(pv_body)
    for hh in range(N_HEADS):
        part = slice(hh * SUBLANES, (hh + 1) * SUBLANES)
        head = slice(hh * HEAD_DIM, (hh + 1) * HEAD_DIM)
        ot_ref[head, :] = ot_ref[head, :] / l_ref[part, :].sum(axis=0, keepdims=True)
    o_ref[...] = ot_ref[...].T.astype(BF16)


def _attn_call(qi3, wit, q, ki3, kb, vt, *, batch, tq, kc, past, n_keys, n_sel):
    tq_total = q.shape[0] // batch
    nq = tq_total // tq
    lp = ki3.shape[1]
    idx_bits = max(1, int(np.ceil(np.log2(lp))))
    d_attn = q.shape[1]
    kernel = functools.partial(_attn_kernel, kc=kc, past=past, n_keys=n_keys, n_sel=n_sel, idx_bits=idx_bits)
    return pl.pallas_call(
        kernel,
        grid=(batch, nq),
        in_specs=[
            pl.BlockSpec((tq, qi3.shape[1]), lambda b, j: (b * nq + j, 0)),
            pl.BlockSpec((1, N_IDX_HEADS, tq), lambda b, j: (b, 0, j)),
            pl.BlockSpec((tq, d_attn), lambda b, j: (b * nq + j, 0)),
            pl.BlockSpec((None, lp, ki3.shape[2]), lambda b, j: (b, 0, 0)),
            pl.BlockSpec((None, lp, kb.shape[2]), lambda b, j: (b, 0, 0)),
            pl.BlockSpec((None, lp // kc, vt.shape[2], kc), lambda b, j: (b, 0, 0, 0)),
        ],
        out_specs=pl.BlockSpec((tq, d_attn), lambda b, j: (b * nq + j, 0)),
        out_shape=jax.ShapeDtypeStruct((batch * tq_total, d_attn), BF16),
        scratch_shapes=[
            pltpu.VMEM((lp, tq), F32),
            pltpu.VMEM((lp, tq), BF16),
            pltpu.VMEM((lp, tq), F32),
            pltpu.VMEM((d_attn, tq), F32),
            pltpu.VMEM((N_HEADS * SUBLANES, tq), F32),
            pltpu.VMEM((N_HEADS * SUBLANES, tq), F32),
            pltpu.VMEM((N_HEADS, lp, tq), F32),
        ],
        compiler_params=pltpu.CompilerParams(
            dimension_semantics=("arbitrary", "arbitrary"), vmem_limit_bytes=VMEM_LIMIT),
        name="attn",
    )(qi3, wit, q, ki3, kb, vt)


def _rope_tables(pos):
    half = ROT_DIM // 2
    inv = ROPE_THETA ** (-np.arange(half, dtype=np.float64) * (2.0 / ROT_DIM))
    ang = np.asarray(pos, np.float64)[:, None] * inv[None, :]
    cos = np.ones((len(pos), HEAD_DIM))
    slo = np.zeros((len(pos), HEAD_DIM))
    shi = np.zeros((len(pos), HEAD_DIM))
    cos[:, :half] = np.cos(ang)
    cos[:, half:ROT_DIM] = np.cos(ang)
    slo[:, :half] = -np.sin(ang)
    shi[:, half:ROT_DIM] = np.sin(ang)
    rep = LANES // HEAD_DIM
    return tuple(jnp.asarray(np.tile(t, (1, rep)), F32) for t in (cos, slo, shi))


def _pad_rows(a, rows):
    return jnp.pad(a, ((0, 0), (0, rows - a.shape[1])) + ((0, 0),) * (a.ndim - 2))


def _layer(x, past, cache, wts, *, tm_ffn, tm_proj, tq, kc):
    (ffn1, projw, woc, woa, ffn2) = wts
    b, t, d = x.shape
    n = b * t
    cache_k, cache_v, cache_ik, conv_prev = cache
    x1 = _ffn_call(x.reshape(n, d), *ffn1, tm=tm_ffn)

    d_conv = conv_prev.shape[-1]
    cprev = jnp.pad(conv_prev, ((0, 0), (SUBLANES - (CONV_W - 1), 0), (0, 0)))
    tables = _rope_tables(past + np.arange(t))
    conv_out, q, k, v, kb, qi3, kiwi, ki3, ulast = _proj_call(
        x1, cprev, projw, tables, tm_proj, t // tm_proj)

    n_keys = past + t
    lp = -(-n_keys // kc) * kc
    kb_all = kb.reshape(b, t, LANES)
    v_all = v.reshape(b, t, LANES).astype(BF16)
    ki3_all = ki3.reshape(b, t, 2 * LANES)
    if past:
        ck = cache_k.reshape(b, past, LANES).astype(BF16)
        cv = cache_v.reshape(b, past, LANES).astype(BF16)
        cik = jnp.pad(cache_ik, ((0, 0), (0, 0), (0, LANES - IDX_DIM))).reshape(b * past, LANES)
        cik3 = _stack_keys_call(cik, past).reshape(b, past, 2 * LANES)
        kb_all = jnp.concatenate([ck, kb_all], axis=1)
        v_all = jnp.concatenate([cv, v_all], axis=1)
        ki3_all = jnp.concatenate([cik3, ki3_all], axis=1)
    kb_all = _pad_rows(kb_all, lp)
    ki3_all = _pad_rows(ki3_all, lp)
    vt = jnp.swapaxes(_pad_rows(v_all, lp).reshape(b, lp // kc, kc, LANES), 2, 3)

    tqp = -(-t // tq) * tq
    wi = kiwi[:, IDX_DIM:IDX_DIM + N_IDX_HEADS].reshape(b, t, N_IDX_HEADS)
    wit = jnp.swapaxes(_pad_rows(wi, tqp), 1, 2)
    if tqp == t:
        padq = lambda a: a
    else:
        padq = lambda a: _pad_rows(a.reshape(b, t, a.shape[-1]), tqp).reshape(b * tqp, a.shape[-1])
    n_sel = max(1, min(TOP_K, n_keys // 4))
    attn = _attn_call(padq(qi3), wit, padq(q), ki3_all, kb_all, vt,
                      batch=b, tq=tq, kc=kc, past=past, n_keys=n_keys, n_sel=n_sel)
    if tqp != t:
        attn = attn.reshape(b, tqp, -1)[:, :t].reshape(n, -1)

    y = _out_ffn_call(x1, conv_out, attn, woc, woa, *ffn2, tm=tm_ffn)

    tiles = t // tm_proj
    conv_state = ulast.reshape(b, tiles, SUBLANES, d_conv)[:, -1, SUBLANES - (CONV_W - 1):]
    return (y.reshape(b, t, d),
            k.reshape(b, t, N_KV_HEADS, HEAD_DIM),
            v.reshape(b, t, N_KV_HEADS, HEAD_DIM),
            kiwi[:, :IDX_DIM].reshape(b, t, IDX_DIM),
            conv_state)


def _prep_weights(l, ffn1_norm, ffn1_w_gate, ffn1_w_up, ffn1_w_down, mix_norm, w_in, conv_w,
                  q_norm, k_norm, idx_k_norm, w_out, ffn2_norm, ffn2_w_gate, ffn2_w_up, ffn2_w_down):
    d_conv = conv_w.shape[-1]
    n_main = 3 * d_conv + N_HEADS * HEAD_DIM + 2 * N_KV_HEADS * HEAD_DIM
    row = lambda g: g[l][None, :]
    ffn1 = (row(ffn1_norm), ffn1_w_gate[l].astype(BF16), ffn1_w_up[l].astype(BF16), ffn1_w_down[l].astype(BF16))
    ffn2 = (row(ffn2_norm), ffn2_w_gate[l].astype(BF16), ffn2_w_up[l].astype(BF16), ffn2_w_down[l].astype(BF16))
    w = w_in[l]
    wmain = w[:, :n_main].astype(BF16)
    widx = w[:, n_main:]
    widx = jnp.pad(widx, ((0, 0), (0, -widx.shape[1] % LANES)))
    wih = widx.astype(BF16)
    wil = (widx - wih.astype(F32)).astype(BF16)
    gq = jnp.tile(q_norm[l], LANES // HEAD_DIM)[None, :]
    gk = jnp.tile(k_norm[l], LANES // HEAD_DIM)[None, :]
    gik = jnp.pad(idx_k_norm[l], (0, LANES - IDX_DIM))[None, :]
    lane = np.arange(LANES)
    bd = jnp.asarray(lane[:, None] // HEAD_DIM == lane[None, :] // HEAD_DIM, BF16)
    projw = (row(mix_norm), wmain, wih, wil, conv_w[l], gq, gk, gik, bd)
    wo = w_out[l].astype(BF16)
    return (ffn1, projw, wo[:d_conv], wo[d_conv:], ffn2)


def kernel(x_prompt, x_sample, cache_k, cache_v, cache_idx_k, state_conv, ffn1_norm, ffn1_w_gate, ffn1_w_up, ffn1_w_down, mix_norm, w_in, conv_w, q_norm, k_norm, idx_k_norm, w_out, ffn2_norm, ffn2_w_gate, ffn2_w_up, ffn2_w_down):
    depth = w_in.shape[0]
    bp, tp, _ = x_prompt.shape
    bs, ts, _ = x_sample.shape
    past = cache_k.shape[2]
    d_conv = conv_w.shape[-1]
    hp, hs = x_prompt, x_sample
    outs_p, outs_s = [], []
    for l in range(depth):
        wts = _prep_weights(l, ffn1_norm, ffn1_w_gate, ffn1_w_up, ffn1_w_down, mix_norm, w_in, conv_w,
                            q_norm, k_norm, idx_k_norm, w_out, ffn2_norm, ffn2_w_gate, ffn2_w_up, ffn2_w_down)
        empty = (None, None, None, jnp.zeros((bp, CONV_W - 1, d_conv), F32))
        hp, *rest_p = _layer(hp, 0, empty, wts, tm_ffn=512, tm_proj=256, tq=256, kc=256)
        cache = (cache_k[l], cache_v[l], cache_idx_k[l], state_conv[l])
        hs, *rest_s = _layer(hs, past, cache, wts, tm_ffn=bs * ts, tm_proj=ts, tq=128, kc=256)
        outs_p.append(rest_p)
        outs_s.append(rest_s)
    stack = lambda outs, i: jnp.stack([o[i] for o in outs])
    return (hp, hs,
            stack(outs_p, 0), stack(outs_p, 1), stack(outs_p, 2), stack(outs_p, 3),
            stack(outs_s, 0), stack(outs_s, 1), stack(outs_s, 2), stack(outs_s, 3))
```

```python
import functools

import numpy as np
import jax
import jax.numpy as jnp
from jax import lax
from jax.experimental import pallas as pl
from jax.experimental.pallas import tpu as pltpu

F32 = jnp.float32
BF16 = jnp.bfloat16
I32 = jnp.int32
I16 = jnp.int16

CHUNK = 64
CONV_W = 3
N_HEADS = 8
N_KV_HEADS = 2
HEAD_DIM = 64
ROT_DIM = 16
ROPE_THETA = 500000.0
N_IDX_HEADS = 8
IDX_DIM = 64
TOP_K = 256
EPS = 1e-6

LANES = 128
SUBLANES = 8
PACKED_ROWS = 16
INT_MIN = -(2 ** 31)
I16_MIN = -(2 ** 15)
LOG2E = 1.4426950408889634
NEG_BIG = -1e30
VMEM_LIMIT = 56 * 1024 * 1024


def _const_spec(shape):
    nd = len(shape)
    return pl.BlockSpec(shape, lambda *_: (0,) * nd, pipeline_mode=pl.Buffered(1))


def _dot(a, b):
    return jnp.dot(a, b, preferred_element_type=F32)


def _dot_nt(a, b):
    return lax.dot_general(a, b, (((1,), (1,)), ((), ())), preferred_element_type=F32)


def _split_bf16(x):
    hi = x.astype(BF16).astype(F32)
    lo = (x - hi).astype(BF16).astype(F32)
    return hi, lo


def _rmsnorm(x, g):
    ms = jnp.mean(x * x, axis=-1, keepdims=True)
    return (x * lax.rsqrt(ms + EPS)) * g


def _swiglu(h, wg_ref, wu_ref, wd_ref):
    g = _dot(h, wg_ref[...])
    u = _dot(h, wu_ref[...])
    a = (g * jax.nn.sigmoid(g)) * u
    return _dot(a.astype(BF16), wd_ref[...])


def _ffn_kernel(x_ref, g_ref, wg_ref, wu_ref, wd_ref, o_ref):
    x = x_ref[...]
    h = _rmsnorm(x, g_ref[...]).astype(BF16)
    o_ref[...] = x + 0.5 * _swiglu(h, wg_ref, wu_ref, wd_ref)


def _ffn_call(x, g, wg, wu, wd, tm):
    n, d = x.shape
    dff = wg.shape[1]
    return pl.pallas_call(
        _ffn_kernel,
        grid=(n // tm,),
        in_specs=[
            pl.BlockSpec((tm, d), lambda i: (i, 0)),
            _const_spec((1, d)),
            _const_spec((d, dff)),
            _const_spec((d, dff)),
            _const_spec((dff, d)),
        ],
        out_specs=pl.BlockSpec((tm, d), lambda i: (i, 0)),
        out_shape=jax.ShapeDtypeStruct((n, d), F32),
        compiler_params=pltpu.CompilerParams(
            dimension_semantics=("arbitrary",), vmem_limit_bytes=VMEM_LIMIT),
        name="ffn",
    )(x, g, wg, wu, wd)


def _out_ffn_kernel(x_ref, c_ref, a_ref, woc_ref, woa_ref, g_ref, wg_ref, wu_ref, wd_ref, o_ref):
    x = x_ref[...] + (_dot(c_ref[...], woc_ref[...]) + _dot(a_ref[...], woa_ref[...]))
    h = _rmsnorm(x, g_ref[...]).astype(BF16)
    o_ref[...] = x + 0.5 * _swiglu(h, wg_ref, wu_ref, wd_ref)


def _out_ffn_call(x, conv_out, attn, woc, woa, g, wg, wu, wd, tm):
    n, d = x.shape
    dff = wg.shape[1]
    dc = conv_out.shape[1]
    da = attn.shape[1]
    return pl.pallas_call(
        _out_ffn_kernel,
        grid=(n // tm,),
        in_specs=[
            pl.BlockSpec((tm, d), lambda i: (i, 0)),
            pl.BlockSpec((tm, dc), lambda i: (i, 0)),
            pl.BlockSpec((tm, da), lambda i: (i, 0)),
            _const_spec((dc, d)),
            _const_spec((da, d)),
            _const_spec((1, d)),
            _const_spec((d, dff)),
            _const_spec((d, dff)),
            _const_spec((dff, d)),
        ],
        out_specs=pl.BlockSpec((tm, d), lambda i: (i, 0)),
        out_shape=jax.ShapeDtypeStruct((n, d), F32),
        compiler_params=pltpu.CompilerParams(
            dimension_semantics=("arbitrary",), vmem_limit_bytes=VMEM_LIMIT),
        name="out_ffn",
    )(x, conv_out, attn, woc, woa, g, wg, wu, wd)


def _rope(y, cos, sin_lo, sin_hi):
    half = ROT_DIM // 2
    return (y * cos + pltpu.roll(y, LANES - half, 1) * sin_lo) + pltpu.roll(y, half, 1) * sin_hi


def _group_sumsq(x, bd):
    hi, lo = _split_bf16(x * x)
    return _dot(hi.astype(BF16), bd) + _dot(lo.astype(BF16), bd)


def _head_norm(x, g, bd):
    ms = _group_sumsq(x, bd) * (1.0 / HEAD_DIM)
    return (x * lax.rsqrt(ms + EPS)) * g


def _stack_keys(kin):
    lane = lax.broadcasted_iota(I32, kin.shape, 1)
    first = lane < IDX_DIM
    hi, lo = _split_bf16(kin)
    c0 = jnp.where(first, hi, pltpu.roll(hi, IDX_DIM, 1))
    c1 = jnp.where(first, lo, 0.0)
    return jnp.concatenate([c0, c1], axis=1).astype(BF16)


def _stack_queries(qc):
    lane = lax.broadcasted_iota(I32, qc.shape, 1)
    first = lane < IDX_DIM
    hi, lo = _split_bf16(qc)
    hi_sw = pltpu.roll(hi, IDX_DIM, 1)
    lo_sw = pltpu.roll(lo, IDX_DIM, 1)
    a0 = jnp.where(first, hi, lo_sw)
    a1 = jnp.where(first, hi, 0.0)
    b0 = jnp.where(first, hi_sw, lo)
    b1 = jnp.where(first, hi_sw, 0.0)
    return jnp.concatenate([a0, a1, b0, b1], axis=1).astype(BF16)


def _proj_kernel(x_ref, gmix_ref, wmain_ref, wih_ref, wil_ref, convw_ref, cprev_ref,
                 gq_ref, gk_ref, gik_ref, cos_ref, slo_ref, shi_ref, bd_ref,
                 convout_ref, q_ref, k_ref, v_ref, kb_ref, vb_ref, qi3_ref, kiwi_ref, ki3_ref, ulast_ref,
                 ubuf_ref, *, tiles_per_batch, d_conv):
    i = pl.program_id(0)
    tm = x_ref.shape[0]

    @pl.when(i % tiles_per_batch == 0)
    def _():
        ubuf_ref[0:SUBLANES, :] = cprev_ref[0]

    h = _rmsnorm(x_ref[...], gmix_ref[...])
    h_hi = h.astype(BF16)
    h_lo = (h - h_hi.astype(F32)).astype(BF16)
    q0 = 3 * d_conv
    nq = N_HEADS * HEAD_DIM
    n_main = wmain_ref.shape[1]
    zq = _dot(h_hi, wmain_ref[:, q0:n_main])
    wih = wih_ref[...]
    zi = (_dot(h_hi, wih) + _dot(h_lo, wih)) + _dot(h_hi, wil_ref[...])

    cos = cos_ref[...]
    slo = slo_ref[...]
    shi = shi_ref[...]
    bd = bd_ref[...]
    scale = HEAD_DIM ** -0.5 * LOG2E
    for c in range(nq // LANES):
        qc = zq[:, c * LANES:(c + 1) * LANES]
        qn = _head_norm(qc, gq_ref[...], bd)
        q_ref[:, c * LANES:(c + 1) * LANES] = (_rope(qn, cos, slo, shi) * scale).astype(BF16)
    kn = _rope(_head_norm(zq[:, nq:nq + LANES], gk_ref[...], bd), cos, slo, shi)
    vn = zq[:, nq + LANES:nq + 2 * LANES]
    kb_ref[...] = kn.astype(BF16)
    vb_ref[...] = vn.astype(BF16)
    for kv in range(N_KV_HEADS):
        k_ref[:, kv, :] = kn[:, kv * HEAD_DIM:(kv + 1) * HEAD_DIM]
        v_ref[:, kv, :] = vn[:, kv * HEAD_DIM:(kv + 1) * HEAD_DIM]

    zc = _dot(h_hi, wmain_ref[:, 0:q0])

    ni = N_IDX_HEADS * IDX_DIM
    for c in range(ni // LANES):
        qic = _rope(zi[:, c * LANES:(c + 1) * LANES], cos, slo, shi)
        qi3_ref[:, c * 4 * LANES:(c + 1) * 4 * LANES] = _stack_queries(qic)
    kw = zi[:, ni:ni + LANES]
    kin = _rope(_head_norm(kw, gik_ref[...], bd), cos, slo, shi)
    lane = lax.broadcasted_iota(I32, kw.shape, 1)
    kiwi_ref[...] = jnp.where(lane < IDX_DIM, kin, kw * (N_IDX_HEADS ** -0.5))
    ki3_ref[...] = _stack_keys(kin)

    gb = zc[:, 0:d_conv]
    u = zc[:, d_conv:2 * d_conv] * zc[:, 2 * d_conv:3 * d_conv]
    ubuf_ref[SUBLANES:SUBLANES + tm, :] = u
    w = convw_ref[...]
    conv = (ubuf_ref[SUBLANES - 2:SUBLANES - 2 + tm, :] * w[0:1, :]
            + ubuf_ref[SUBLANES - 1:SUBLANES - 1 + tm, :] * w[1:2, :]) + u * w[2:3, :]
    convout_ref[...] = (gb * conv).astype(BF16)
    tail = u[tm - SUBLANES:tm, :]
    ulast_ref[0] = tail
    ubuf_ref[0:SUBLANES, :] = tail


def _proj_call(x1, cprev, wts, tables, tm, tiles_per_batch):
    n, d = x1.shape
    (gmix, wmain, wih, wil, convw, gq, gk, gik, bd) = wts
    cos, slo, shi = tables
    d_conv = convw.shape[1]
    nt = n // tm
    row = lambda w: pl.BlockSpec((tm, w), lambda i: (i, 0))
    tab = pl.BlockSpec((tm, LANES), lambda i: (i % tiles_per_batch, 0))
    cache_rows = pl.BlockSpec((tm, N_KV_HEADS, HEAD_DIM), lambda i: (i, 0, 0))
    nq = N_HEADS * HEAD_DIM
    out_shape = (
        jax.ShapeDtypeStruct((n, d_conv), BF16),
        jax.ShapeDtypeStruct((n, nq), BF16),
        jax.ShapeDtypeStruct((n, N_KV_HEADS, HEAD_DIM), F32),
        jax.ShapeDtypeStruct((n, N_KV_HEADS, HEAD_DIM), F32),
        jax.ShapeDtypeStruct((n, LANES), BF16),
        jax.ShapeDtypeStruct((n, LANES), BF16),
        jax.ShapeDtypeStruct((n, N_IDX_HEADS * 2 * LANES), BF16),
        jax.ShapeDtypeStruct((n, LANES), F32),
        jax.ShapeDtypeStruct((n, 2 * LANES), BF16),
        jax.ShapeDtypeStruct((nt, SUBLANES, d_conv), F32),
    )
    out_specs = (
        row(d_conv), row(nq), cache_rows, cache_rows, row(LANES), row(LANES),
        row(N_IDX_HEADS * 2 * LANES), row(LANES), row(2 * LANES),
        pl.BlockSpec((1, SUBLANES, d_conv), lambda i: (i, 0, 0)),
    )
    in_specs = [
        row(d),
        _const_spec(gmix.shape), _const_spec(wmain.shape), _const_spec(wih.shape), _const_spec(wil.shape),
        _const_spec(convw.shape),
        pl.BlockSpec((1, SUBLANES, d_conv), lambda i: (i // tiles_per_batch, 0, 0)),
        _const_spec(gq.shape), _const_spec(gk.shape), _const_spec(gik.shape),
        tab, tab, tab,
        _const_spec(bd.shape),
    ]
    return pl.pallas_call(
        functools.partial(_proj_kernel, tiles_per_batch=tiles_per_batch, d_conv=d_conv),
        grid=(nt,),
        in_specs=in_specs,
        out_specs=out_specs,
        out_shape=out_shape,
        scratch_shapes=[pltpu.VMEM((tm + SUBLANES, d_conv), F32)],
        compiler_params=pltpu.CompilerParams(
            dimension_semantics=("arbitrary",), vmem_limit_bytes=VMEM_LIMIT),
        name="proj",
    )(x1, gmix, wmain, wih, wil, convw, cprev, gq, gk, gik, cos, slo, shi, bd)


def _stack_keys_kernel(k_ref, o_ref):
    o_ref[...] = _stack_keys(k_ref[...])


def _stack_keys_call(kpad, tm):
    n = kpad.shape[0]
    return pl.pallas_call(
        _stack_keys_kernel,
        grid=(n // tm,),
        in_specs=[pl.BlockSpec((tm, LANES), lambda i: (i, 0))],
        out_specs=pl.BlockSpec((tm, 2 * LANES), lambda i: (i, 0)),
        out_shape=jax.ShapeDtypeStruct((n, 2 * LANES), BF16),
        name="stack_keys",
    )(kpad)


def _attn_kernel(qi3_ref, wit_ref, q_ref, ki3_ref, kb_ref, vt_ref, o_ref,
                 score_ref, sb_ref, bias_ref, ot_ref, m_ref, l_ref, s_ref, *, kc, past, n_keys, n_sel, idx_bits):
    tq = q_ref.shape[0]
    j = pl.program_id(1)
    q_first = past + j * tq
    pos = q_first + lax.broadcasted_iota(I32, (1, tq), 1)
    limit = jnp.minimum(((pos >> 6) + 1) * CHUNK, n_keys)
    max_limit = jnp.minimum((((q_first + tq - 1) >> 6) + 1) * CHUNK, n_keys)
    nk = (max_limit + kc - 1) // kc

    def for_chunk_pairs(body):
        def step(i, carry):
            body(2 * i, 2)
            return carry

        lax.fori_loop(0, nk // 2, step, 0)

        @pl.when(nk % 2 == 1)
        def _():
            body(nk - 1, 1)

    def slab(c, n):
        return pl.ds(pl.multiple_of(c * kc, kc), n * kc)

    def slab_pos(c, n):
        return c * kc + lax.broadcasted_iota(I32, (n * kc, tq), 0)

    def rows(c):
        return slab(c, 1)

    def key_pos(c):
        return slab_pos(c, 1)

    w8 = wit_ref[0] * (IDX_DIM ** -0.5)

    def score_body(c, n):
        at = slab(c, n)
        kk = ki3_ref[at, :]
        for hh in range(N_IDX_HEADS):
            d = _dot_nt(kk, qi3_ref[:, hh * 2 * LANES:(hh + 1) * 2 * LANES])
            term = jnp.maximum(d, 0.0) * w8[hh:hh + 1, :]
            score_ref[at, :] = term if hh == 0 else score_ref[at, :] + term
        score = jnp.where(slab_pos(c, n) < limit, score_ref[at, :] + 0.0, -jnp.inf)
        score_ref[at, :] = score
        sb_ref[at, :] = score.astype(BF16)

    for_chunk_pairs(score_body)

    def count(pred):
        def body(c, acc):
            m = pred(c, score_ref[rows(c), :]).astype(I32)
            return acc + m.reshape(kc // SUBLANES, SUBLANES, tq).sum(axis=0)
        acc = lax.fori_loop(0, nk, body, jnp.zeros((SUBLANES, tq), I32))
        return acc.sum(axis=0, keepdims=True)

    def as_float(t):
        return pltpu.bitcast(t ^ ((t >> 31) & 0x7FFFFFFF), F32)

    def as_bf16_tile(t):
        bits = t ^ ((t >> 15) & 0x7FFF)
        return pltpu.bitcast(jnp.broadcast_to(bits, (PACKED_ROWS, tq)).astype(I16), BF16)

    def count_coarse(t):
        cand = as_bf16_tile(t)
        one = jnp.ones((PACKED_ROWS, tq), I16)
        zero = jnp.zeros((PACKED_ROWS, tq), I16)

        def body(c, acc):
            s = sb_ref[rows(c), :]
            for r in range(kc // PACKED_ROWS):
                hit = s[r * PACKED_ROWS:(r + 1) * PACKED_ROWS, :] >= cand
                acc = acc + jnp.where(hit, one, zero)
            return acc

        acc = lax.fori_loop(0, nk, body, zero)
        return acc.astype(I32).sum(axis=0, keepdims=True)

    c_nonneg = count_coarse(jnp.zeros((1, tq), I32))
    t1 = jnp.where(c_nonneg >= n_sel, 0, I16_MIN).astype(I32)

    def coarse_body(b, t):
        cand = t | (jnp.int32(1) << (14 - b))
        return jnp.where(count_coarse(cand) >= n_sel, cand, t)

    t1 = lax.fori_loop(0, 15, coarse_body, t1)
    live = t1 != I16_MIN
    image1 = t1 * 65536 + jnp.where(t1 < 0, 0xFFFF, 0)
    base = image1 - 0x8001

    def fine_body(b, carry):
        t, c_at = carry
        cand = t + (jnp.int32(1) << (16 - b))
        cand_f = as_float(cand)
        cnt = count(lambda c, s: s >= cand_f)
        ok = cnt >= n_sel
        return jnp.where(ok, cand, t), jnp.where(ok, cnt, c_at)

    t_sel, c_ge = lax.fori_loop(0, 17, fine_body, (base, jnp.full((1, tq), n_sel, I32)))
    thr = jnp.where(live, as_float(t_sel), -jnp.inf)

    surplus = jnp.max(jnp.where(live & (c_ge > n_sel), 1, 0))

    def tie_search():
        room = n_sel - count(lambda c, s: s > thr)

        def body(b, jm):
            cand = jm | (jnp.int32(1) << (idx_bits - 1 - b))
            cnt = count(lambda c, s: (s == thr) & (key_pos(c) < cand))
            return jnp.where(cnt < room, cand, jm)
        return lax.fori_loop(0, idx_bits, body, jnp.zeros((1, tq), I32))

    jm = lax.cond(surplus > 0, tie_search, lambda: jnp.full((1, tq), 2 ** idx_bits, I32))
    jm = jnp.where(live, jm, -1)

    def bias_body(c, carry):
        s = score_ref[rows(c), :]
        sel = (s > thr) | ((s == thr) & (key_pos(c) <= jm))
        bias_ref[rows(c), :] = jnp.where(sel, 0.0, NEG_BIG)
        return carry

    lax.fori_loop(0, nk, bias_body, 0)

    group = N_HEADS // N_KV_HEADS
    m_ref[...] = jnp.full(m_ref.shape, NEG_BIG, F32)
    l_ref[...] = jnp.zeros(l_ref.shape, F32)
    ot_ref[...] = jnp.zeros(ot_ref.shape, F32)

    def fold(x, op):
        return op(x.reshape(x.shape[0] // SUBLANES, SUBLANES, tq), axis=0)

    def logit_body(c, n):
        at = slab(c, n)
        for hh in range(N_HEADS):
            kv = hh // group
            part = slice(hh * SUBLANES, (hh + 1) * SUBLANES)
            kk = kb_ref[at, kv * HEAD_DIM:(kv + 1) * HEAD_DIM]
            s = _dot_nt(kk, q_ref[:, hh * HEAD_DIM:(hh + 1) * HEAD_DIM]) + bias_ref[at, :]
            s_ref[hh, at, :] = s
            m_ref[part, :] = jnp.maximum(m_ref[part, :], fold(s, jnp.max))

    for_chunk_pairs(logit_body)

    def pv_body(c, n):
        at = slab(c, n)
        for hh in range(N_HEADS):
            kv = hh // group
            part = slice(hh * SUBLANES, (hh + 1) * SUBLANES)
            head = slice(hh * HEAD_DIM, (hh + 1) * HEAD_DIM)
            m = m_ref[part, :].max(axis=0, keepdims=True)
            p = jnp.exp2(s_ref[hh, at, :] - m)
            l_ref[part, :] = l_ref[part, :] + fold(p, jnp.sum)
            vt = jnp.concatenate(
                [vt_ref[c + k, kv * HEAD_DIM:(kv + 1) * HEAD_DIM, :] for k in range(n)], axis=1)
            ot_ref[head, :] = ot_ref[head, :] + _dot(vt, p.astype(BF16))

    for_chunk_pairs(pv_body)
    for hh in range(N_HEADS):
        part = slice(hh * SUBLANES, (hh + 1) * SUBLANES)
        head = slice(hh * HEAD_DIM, (hh + 1) * HEAD_DIM)
        ot_ref[head, :] = ot_ref[head, :] / l_ref[part, :].sum(axis=0, keepdims=True)
    o_ref[...] = ot_ref[...].T.astype(BF16)


def _attn_call(qi3, wit, q, ki3, kb, vt, *, batch, tq, kc, past, n_keys, n_sel):
    tq_total = q.shape[0] // batch
    nq = tq_total // tq
    lp = ki3.shape[1]
    idx_bits = max(1, int(np.ceil(np.log2(lp))))
    d_attn = q.shape[1]
    kernel = functools.partial(_attn_kernel, kc=kc, past=past, n_keys=n_keys, n_sel=n_sel, idx_bits=idx_bits)
    return pl.pallas_call(
        kernel,
        grid=(batch, nq),
        in_specs=[
            pl.BlockSpec((tq, qi3.shape[1]), lambda b, j: (b * nq + j, 0)),
            pl.BlockSpec((1, N_IDX_HEADS, tq), lambda b, j: (b, 0, j)),
            pl.BlockSpec((tq, d_attn), lambda b, j: (b * nq + j, 0)),
            pl.BlockSpec((None, lp, ki3.shape[2]), lambda b, j: (b, 0, 0)),
            pl.BlockSpec((None, lp, kb.shape[2]), lambda b, j: (b, 0, 0)),
            pl.BlockSpec((None, lp // kc, vt.shape[2], kc), lambda b, j: (b, 0, 0, 0)),
        ],
        out_specs=pl.BlockSpec((tq, d_attn), lambda b, j: (b * nq + j, 0)),
        out_shape=jax.ShapeDtypeStruct((batch * tq_total, d_attn), BF16),
        scratch_shapes=[
            pltpu.VMEM((lp, tq), F32),
            pltpu.VMEM((lp, tq), BF16),
            pltpu.VMEM((lp, tq), F32),
            pltpu.VMEM((d_attn, tq), F32),
            pltpu.VMEM((N_HEADS * SUBLANES, tq), F32),
            pltpu.VMEM((N_HEADS * SUBLANES, tq), F32),
            pltpu.VMEM((N_HEADS, lp, tq), F32),
        ],
        compiler_params=pltpu.CompilerParams(
            dimension_semantics=("arbitrary", "arbitrary"), vmem_limit_bytes=VMEM_LIMIT),
        name="attn",
    )(qi3, wit, q, ki3, kb, vt)


def _rope_tables(pos):
    half = ROT_DIM // 2
    inv = ROPE_THETA ** (-np.arange(half, dtype=np.float64) * (2.0 / ROT_DIM))
    ang = np.asarray(pos, np.float64)[:, None] * inv[None, :]
    cos = np.ones((len(pos), HEAD_DIM))
    slo = np.zeros((len(pos), HEAD_DIM))
    shi = np.zeros((len(pos), HEAD_DIM))
    cos[:, :half] = np.cos(ang)
    cos[:, half:ROT_DIM] = np.cos(ang)
    slo[:, :half] = -np.sin(ang)
    shi[:, half:ROT_DIM] = np.sin(ang)
    rep = LANES // HEAD_DIM
    return tuple(jnp.asarray(np.tile(t, (1, rep)), F32) for t in (cos, slo, shi))


def _pad_rows(a, rows):
    return jnp.pad(a, ((0, 0), (0, rows - a.shape[1])) + ((0, 0),) * (a.ndim - 2))


def _layer(x, past, cache, wts, *, tm_ffn, tm_proj, tq, kc):
    (ffn1, projw, woc, woa, ffn2) = wts
    b, t, d = x.shape
    n = b * t
    cache_k, cache_v, cache_ik, conv_prev = cache
    x1 = _ffn_call(x.reshape(n, d), *ffn1, tm=tm_ffn)

    d_conv = conv_prev.shape[-1]
    cprev = jnp.pad(conv_prev, ((0, 0), (SUBLANES - (CONV_W - 1), 0), (0, 0)))
    tables = _rope_tables(past + np.arange(t))
    conv_out, q, k, v, kb, vb, qi3, kiwi, ki3, ulast = _proj_call(
        x1, cprev, projw, tables, tm_proj, t // tm_proj)

    n_keys = past + t
    lp = -(-n_keys // kc) * kc
    kb_all = kb.reshape(b, t, LANES)
    v_all = vb.reshape(b, t, LANES)
    ki3_all = ki3.reshape(b, t, 2 * LANES)
    if past:
        ck = cache_k.reshape(b, past, LANES).astype(BF16)
        cv = cache_v.reshape(b, past, LANES).astype(BF16)
        cik = jnp.pad(cache_ik, ((0, 0), (0, 0), (0, LANES - IDX_DIM))).reshape(b * past, LANES)
        cik3 = _stack_keys_call(cik, past).reshape(b, past, 2 * LANES)
        kb_all = jnp.concatenate([ck, kb_all], axis=1)
        v_all = jnp.concatenate([cv, v_all], axis=1)
        ki3_all = jnp.concatenate([cik3, ki3_all], axis=1)
    kb_all = _pad_rows(kb_all, lp)
    ki3_all = _pad_rows(ki3_all, lp)
    vt = jnp.swapaxes(_pad_rows(v_all, lp).reshape(b, lp // kc, kc, LANES), 2, 3)

    tqp = -(-t // tq) * tq
    wi = kiwi[:, IDX_DIM:IDX_DIM + N_IDX_HEADS].reshape(b, t, N_IDX_HEADS)
    wit = jnp.swapaxes(_pad_rows(wi, tqp), 1, 2)
    if tqp == t:
        padq = lambda a: a
    else:
        padq = lambda a: _pad_rows(a.reshape(b, t, a.shape[-1]), tqp).reshape(b * tqp, a.shape[-1])
    n_sel = max(1, min(TOP_K, n_keys // 4))
    attn = _attn_call(padq(qi3), wit, padq(q), ki3_all, kb_all, vt,
                      batch=b, tq=tq, kc=kc, past=past, n_keys=n_keys, n_sel=n_sel)
    if tqp != t:
        attn = attn.reshape(b, tqp, -1)[:, :t].reshape(n, -1)

    y = _out_ffn_call(x1, conv_out, attn, woc, woa, *ffn2, tm=tm_ffn)

    tiles = t // tm_proj
    conv_state = ulast.reshape(b, tiles, SUBLANES, d_conv)[:, -1, SUBLANES - (CONV_W - 1):]
    return (y.reshape(b, t, d),
            k.reshape(b, t, N_KV_HEADS, HEAD_DIM),
            v.reshape(b, t, N_KV_HEADS, HEAD_DIM),
            kiwi[:, :IDX_DIM].reshape(b, t, IDX_DIM),
            conv_state)


def _prep_weights(l, ffn1_norm, ffn1_w_gate, ffn1_w_up, ffn1_w_down, mix_norm, w_in, conv_w,
                  q_norm, k_norm, idx_k_norm, w_out, ffn2_norm, ffn2_w_gate, ffn2_w_up, ffn2_w_down):
    d_conv = conv_w.shape[-1]
    n_main = 3 * d_conv + N_HEADS * HEAD_DIM + 2 * N_KV_HEADS * HEAD_DIM
    row = lambda g: g[l][None, :]
    ffn1 = (row(ffn1_norm), ffn1_w_gate[l].astype(BF16), ffn1_w_up[l].astype(BF16), ffn1_w_down[l].astype(BF16))
    ffn2 = (row(ffn2_norm), ffn2_w_gate[l].astype(BF16), ffn2_w_up[l].astype(BF16), ffn2_w_down[l].astype(BF16))
    w = w_in[l]
    wmain = w[:, :n_main].astype(BF16)
    widx = w[:, n_main:]
    widx = jnp.pad(widx, ((0, 0), (0, -widx.shape[1] % LANES)))
    wih = widx.astype(BF16)
    wil = (widx - wih.astype(F32)).astype(BF16)
    gq = jnp.tile(q_norm[l], LANES // HEAD_DIM)[None, :]
    gk = jnp.tile(k_norm[l], LANES // HEAD_DIM)[None, :]
    gik = jnp.pad(idx_k_norm[l], (0, LANES - IDX_DIM))[None, :]
    lane = np.arange(LANES)
    bd = jnp.asarray(lane[:, None] // HEAD_DIM == lane[None, :] // HEAD_DIM, BF16)
    projw = (row(mix_norm), wmain, wih, wil, conv_w[l], gq, gk, gik, bd)
    wo = w_out[l].astype(BF16)
    return (ffn1, projw, wo[:d_conv], wo[d_conv:], ffn2)


def kernel(x_prompt, x_sample, cache_k, cache_v, cache_idx_k, state_conv, ffn1_norm, ffn1_w_gate, ffn1_w_up, ffn1_w_down, mix_norm, w_in, conv_w, q_norm, k_norm, idx_k_norm, w_out, ffn2_norm, ffn2_w_gate, ffn2_w_up, ffn2_w_down):
    depth = w_in.shape[0]
    bp, tp, _ = x_prompt.shape
    bs, ts, _ = x_sample.shape
    past = cache_k.shape[2]
    d_conv = conv_w.shape[-1]
    hp, hs = x_prompt, x_sample
    outs_p, outs_s = [], []
    for l in range(depth):
        wts = _prep_weights(l, ffn1_norm, ffn1_w_gate, ffn1_w_up, ffn1_w_down, mix_norm, w_in, conv_w,
                            q_norm, k_norm, idx_k_norm, w_out, ffn2_norm, ffn2_w_gate, ffn2_w_up, ffn2_w_down)
        empty = (None, None, None, jnp.zeros((bp, CONV_W - 1, d_conv), F32))
        hp, *rest_p = _layer(hp, 0, empty, wts, tm_ffn=512, tm_proj=256, tq=256, kc=256)
        cache = (cache_k[l], cache_v[l], cache_idx_k[l], state_conv[l])
        hs, *rest_s = _layer(hs, past, cache, wts, tm_ffn=bs * ts, tm_proj=ts, tq=128, kc=256)
        outs_p.append(rest_p)
        outs_s.append(rest_s)
    stack = lambda outs, i: jnp.stack([o[i] for o in outs])
    return (hp, hs,
            stack(outs_p, 0), stack(outs_p, 1), stack(outs_p, 2), stack(outs_p, 3),
            stack(outs_s, 0), stack(outs_s, 1), stack(outs_s, 2), stack(outs_s, 3))
```

```python
import functools

import numpy as np
import jax
import jax.numpy as jnp
from jax import lax
from jax.experimental import pallas as pl
from jax.experimental.pallas import tpu as pltpu

F32 = jnp.float32
BF16 = jnp.bfloat16
I32 = jnp.int32
I16 = jnp.int16

CHUNK = 64
CHUNK_SHIFT = CHUNK.bit_length() - 1
assert CHUNK == 1 << CHUNK_SHIFT
CONV_W = 3
N_HEADS = 8
N_KV_HEADS = 2
HEAD_DIM = 64
ROT_DIM = 16
ROPE_THETA = 500000.0
N_IDX_HEADS = 8
IDX_DIM = 64
TOP_K = 256
EPS = 1e-6

LANES = 128
SUBLANES = 8
PACKED_ROWS = 16
INT_MIN = -(2 ** 31)
I16_MIN = -(2 ** 15)
LOG2E = 1.4426950408889634
NEG_BIG = -1e30
VMEM_LIMIT = 56 * 1024 * 1024


def _const_spec(shape):
    nd = len(shape)
    return pl.BlockSpec(shape, lambda *_: (0,) * nd, pipeline_mode=pl.Buffered(1))


def _dot(a, b):
    return jnp.dot(a, b, preferred_element_type=F32)


def _dot_nt(a, b):
    return lax.dot_general(a, b, (((1,), (1,)), ((), ())), preferred_element_type=F32)


def _split_bf16(x):
    hi = x.astype(BF16).astype(F32)
    lo = (x - hi).astype(BF16).astype(F32)
    return hi, lo


def _rmsnorm(x, g):
    ms = jnp.mean(x * x, axis=-1, keepdims=True)
    return (x * lax.rsqrt(ms + EPS)) * g


def _swiglu(h, wg_ref, wu_ref, wd_ref):
    g = _dot(h, wg_ref[...])
    u = _dot(h, wu_ref[...])
    a = (g * jax.nn.sigmoid(g)) * u
    return _dot(a.astype(BF16), wd_ref[...])


def _ffn_kernel(x_ref, g_ref, wg_ref, wu_ref, wd_ref, o_ref):
    x = x_ref[...]
    h = _rmsnorm(x, g_ref[...]).astype(BF16)
    o_ref[...] = x + 0.5 * _swiglu(h, wg_ref, wu_ref, wd_ref)


def _ffn_call(x, g, wg, wu, wd, tm):
    n, d = x.shape
    dff = wg.shape[1]
    return pl.pallas_call(
        _ffn_kernel,
        grid=(n // tm,),
        in_specs=[
            pl.BlockSpec((tm, d), lambda i: (i, 0)),
            _const_spec((1, d)),
            _const_spec((d, dff)),
            _const_spec((d, dff)),
            _const_spec((dff, d)),
        ],
        out_specs=pl.BlockSpec((tm, d), lambda i: (i, 0)),
        out_shape=jax.ShapeDtypeStruct((n, d), F32),
        compiler_params=pltpu.CompilerParams(
            dimension_semantics=("arbitrary",), vmem_limit_bytes=VMEM_LIMIT),
        name="ffn",
    )(x, g, wg, wu, wd)


def _out_ffn_kernel(x_ref, c_ref, a_ref, woc_ref, woa_ref, g_ref, wg_ref, wu_ref, wd_ref, o_ref):
    x = x_ref[...] + (_dot(c_ref[...], woc_ref[...]) + _dot(a_ref[...], woa_ref[...]))
    h = _rmsnorm(x, g_ref[...]).astype(BF16)
    o_ref[...] = x + 0.5 * _swiglu(h, wg_ref, wu_ref, wd_ref)


def _out_ffn_call(x, conv_out, attn, woc, woa, g, wg, wu, wd, tm):
    n, d = x.shape
    dff = wg.shape[1]
    dc = conv_out.shape[1]
    da = attn.shape[1]
    return pl.pallas_call(
        _out_ffn_kernel,
        grid=(n // tm,),
        in_specs=[
            pl.BlockSpec((tm, d), lambda i: (i, 0)),
            pl.BlockSpec((tm, dc), lambda i: (i, 0)),
            pl.BlockSpec((tm, da), lambda i: (i, 0)),
            _const_spec((dc, d)),
            _const_spec((da, d)),
            _const_spec((1, d)),
            _const_spec((d, dff)),
            _const_spec((d, dff)),
            _const_spec((dff, d)),
        ],
        out_specs=pl.BlockSpec((tm, d), lambda i: (i, 0)),
        out_shape=jax.ShapeDtypeStruct((n, d), F32),
        compiler_params=pltpu.CompilerParams(
            dimension_semantics=("arbitrary",), vmem_limit_bytes=VMEM_LIMIT),
        name="out_ffn",
    )(x, conv_out, attn, woc, woa, g, wg, wu, wd)


def _rope(y, cos, sin_lo, sin_hi):
    half = ROT_DIM // 2
    return (y * cos + pltpu.roll(y, LANES - half, 1) * sin_lo) + pltpu.roll(y, half, 1) * sin_hi


def _group_sumsq(x, bd):
    hi, lo = _split_bf16(x * x)
    return _dot(hi.astype(BF16), bd) + _dot(lo.astype(BF16), bd)


def _head_norm(x, g, bd):
    ms = _group_sumsq(x, bd) * (1.0 / HEAD_DIM)
    return (x * lax.rsqrt(ms + EPS)) * g


def _stack_keys(kin):
    lane = lax.broadcasted_iota(I32, kin.shape, 1)
    first = lane < IDX_DIM
    hi, lo = _split_bf16(kin)
    c0 = jnp.where(first, hi, pltpu.roll(hi, IDX_DIM, 1))
    c1 = jnp.where(first, lo, 0.0)
    return jnp.concatenate([c0, c1], axis=1).astype(BF16)


def _stack_queries(qc):
    lane = lax.broadcasted_iota(I32, qc.shape, 1)
    first = lane < IDX_DIM
    hi, lo = _split_bf16(qc)
    hi_sw = pltpu.roll(hi, IDX_DIM, 1)
    lo_sw = pltpu.roll(lo, IDX_DIM, 1)
    a0 = jnp.where(first, hi, lo_sw)
    a1 = jnp.where(first, hi, 0.0)
    b0 = jnp.where(first, hi_sw, lo)
    b1 = jnp.where(first, hi_sw, 0.0)
    return jnp.concatenate([a0, a1, b0, b1], axis=1).astype(BF16)


def _proj_kernel(x_ref, gmix_ref, wmain_ref, wih_ref, wil_ref, convw_ref, cprev_ref,
                 gq_ref, gk_ref, gik_ref, cos_ref, slo_ref, shi_ref, bd_ref,
                 convout_ref, q_ref, k_ref, v_ref, kb_ref, vb_ref, qi3_ref, kiwi_ref, ki3_ref, ulast_ref,
                 ubuf_ref, *, tiles_per_batch, d_conv):
    i = pl.program_id(0)
    tm = x_ref.shape[0]

    @pl.when(i % tiles_per_batch == 0)
    def _():
        ubuf_ref[0:SUBLANES, :] = cprev_ref[0]

    h = _rmsnorm(x_ref[...], gmix_ref[...])
    h_hi = h.astype(BF16)
    h_lo = (h - h_hi.astype(F32)).astype(BF16)
    q0 = 3 * d_conv
    nq = N_HEADS * HEAD_DIM
    n_main = wmain_ref.shape[1]
    zq = _dot(h_hi, wmain_ref[:, q0:n_main])
    wih = wih_ref[...]
    zi = (_dot(h_hi, wih) + _dot(h_lo, wih)) + _dot(h_hi, wil_ref[...])

    cos = cos_ref[...]
    slo = slo_ref[...]
    shi = shi_ref[...]
    bd = bd_ref[...]
    scale = HEAD_DIM ** -0.5 * LOG2E
    for c in range(nq // LANES):
        qc = zq[:, c * LANES:(c + 1) * LANES]
        qn = _head_norm(qc, gq_ref[...], bd)
        q_ref[:, c * LANES:(c + 1) * LANES] = (_rope(qn, cos, slo, shi) * scale).astype(BF16)
    kn = _rope(_head_norm(zq[:, nq:nq + LANES], gk_ref[...], bd), cos, slo, shi)
    vn = zq[:, nq + LANES:nq + 2 * LANES]
    kb_ref[...] = kn.astype(BF16)
    vb_ref[...] = vn.astype(BF16)
    for kv in range(N_KV_HEADS):
        k_ref[:, kv, :] = kn[:, kv * HEAD_DIM:(kv + 1) * HEAD_DIM]
        v_ref[:, kv, :] = vn[:, kv * HEAD_DIM:(kv + 1) * HEAD_DIM]

    zc = _dot(h_hi, wmain_ref[:, 0:q0])

    ni = N_IDX_HEADS * IDX_DIM
    for c in range(ni // LANES):
        qic = _rope(zi[:, c * LANES:(c + 1) * LANES], cos, slo, shi)
        qi3_ref[:, c * 4 * LANES:(c + 1) * 4 * LANES] = _stack_queries(qic)
    kw = zi[:, ni:ni + LANES]
    kin = _rope(_head_norm(kw, gik_ref[...], bd), cos, slo, shi)
    lane = lax.broadcasted_iota(I32, kw.shape, 1)
    kiwi_ref[...] = jnp.where(lane < IDX_DIM, kin, kw * (N_IDX_HEADS ** -0.5))
    ki3_ref[...] = _stack_keys(kin)

    gb = zc[:, 0:d_conv]
    u = zc[:, d_conv:2 * d_conv] * zc[:, 2 * d_conv:3 * d_conv]
    ubuf_ref[SUBLANES:SUBLANES + tm, :] = u
    w = convw_ref[...]
    conv = (ubuf_ref[SUBLANES - 2:SUBLANES - 2 + tm, :] * w[0:1, :]
            + ubuf_ref[SUBLANES - 1:SUBLANES - 1 + tm, :] * w[1:2, :]) + u * w[2:3, :]
    convout_ref[...] = (gb * conv).astype(BF16)
    tail = u[tm - SUBLANES:tm, :]
    ulast_ref[0] = tail
    ubuf_ref[0:SUBLANES, :] = tail


def _proj_call(x1, cprev, wts, tables, tm, tiles_per_batch):
    n, d = x1.shape
    (gmix, wmain, wih, wil, convw, gq, gk, gik, bd) = wts
    cos, slo, shi = tables
    d_conv = convw.shape[1]
    nt = n // tm
    row = lambda w: pl.BlockSpec((tm, w), lambda i: (i, 0))
    tab = pl.BlockSpec((tm, LANES), lambda i: (i % tiles_per_batch, 0))
    cache_rows = pl.BlockSpec((tm, N_KV_HEADS, HEAD_DIM), lambda i: (i, 0, 0))
    nq = N_HEADS * HEAD_DIM
    out_shape = (
        jax.ShapeDtypeStruct((n, d_conv), BF16),
        jax.ShapeDtypeStruct((n, nq), BF16),
        jax.ShapeDtypeStruct((n, N_KV_HEADS, HEAD_DIM), F32),
        jax.ShapeDtypeStruct((n, N_KV_HEADS, HEAD_DIM), F32),
        jax.ShapeDtypeStruct((n, LANES), BF16),
        jax.ShapeDtypeStruct((n, LANES), BF16),
        jax.ShapeDtypeStruct((n, N_IDX_HEADS * 2 * LANES), BF16),
        jax.ShapeDtypeStruct((n, LANES), F32),
        jax.ShapeDtypeStruct((n, 2 * LANES), BF16),
        jax.ShapeDtypeStruct((nt, SUBLANES, d_conv), F32),
    )
    out_specs = (
        row(d_conv), row(nq), cache_rows, cache_rows, row(LANES), row(LANES),
        row(N_IDX_HEADS * 2 * LANES), row(LANES), row(2 * LANES),
        pl.BlockSpec((1, SUBLANES, d_conv), lambda i: (i, 0, 0)),
    )
    in_specs = [
        row(d),
        _const_spec(gmix.shape), _const_spec(wmain.shape), _const_spec(wih.shape), _const_spec(wil.shape),
        _const_spec(convw.shape),
        pl.BlockSpec((1, SUBLANES, d_conv), lambda i: (i // tiles_per_batch, 0, 0)),
        _const_spec(gq.shape), _const_spec(gk.shape), _const_spec(gik.shape),
        tab, tab, tab,
        _const_spec(bd.shape),
    ]
    return pl.pallas_call(
        functools.partial(_proj_kernel, tiles_per_batch=tiles_per_batch, d_conv=d_conv),
        grid=(nt,),
        in_specs=in_specs,
        out_specs=out_specs,
        out_shape=out_shape,
        scratch_shapes=[pltpu.VMEM((tm + SUBLANES, d_conv), F32)],
        compiler_params=pltpu.CompilerParams(
            dimension_semantics=("arbitrary",), vmem_limit_bytes=VMEM_LIMIT),
        name="proj",
    )(x1, gmix, wmain, wih, wil, convw, cprev, gq, gk, gik, cos, slo, shi, bd)


def _stack_keys_kernel(k_ref, o_ref):
    o_ref[...] = _stack_keys(k_ref[...])


def _stack_keys_call(kpad, tm):
    n = kpad.shape[0]
    return pl.pallas_call(
        _stack_keys_kernel,
        grid=(n // tm,),
        in_specs=[pl.BlockSpec((tm, LANES), lambda i: (i, 0))],
        out_specs=pl.BlockSpec((tm, 2 * LANES), lambda i: (i, 0)),
        out_shape=jax.ShapeDtypeStruct((n, 2 * LANES), BF16),
        name="stack_keys",
    )(kpad)


def _attn_kernel(qi3_ref, wit_ref, q_ref, ki3_ref, kb_ref, vt_ref, o_ref,
                 score_ref, sb_ref, bias_ref, ot_ref, m_ref, l_ref, s_ref, *, kc, past, n_keys, n_sel, idx_bits):
    tq = q_ref.shape[0]
    j = pl.program_id(1)
    q_first = past + j * tq
    pos = q_first + lax.broadcasted_iota(I32, (1, tq), 1)
    limit = jnp.minimum(((pos >> CHUNK_SHIFT) + 1) * CHUNK, n_keys)
    max_limit = jnp.minimum((((q_first + tq - 1) >> CHUNK_SHIFT) + 1) * CHUNK, n_keys)
    nk = (max_limit + kc - 1) // kc

    def for_chunk_slabs(body):
        def step(i, carry):
            body(4 * i, 4)
            return carry

        lax.fori_loop(0, nk // 4, step, 0)
        rest = nk % 4

        @pl.when(rest >= 2)
        def _():
            body(nk - rest, 2)

        @pl.when(rest % 2 == 1)
        def _():
            body(nk - 1, 1)

    def slab(c, n):
        return pl.ds(pl.multiple_of(c * kc, kc), n * kc)

    def slab_pos(c, n):
        return c * kc + lax.broadcasted_iota(I32, (n * kc, tq), 0)

    def rows(c):
        return slab(c, 1)

    def key_pos(c):
        return slab_pos(c, 1)

    w8 = wit_ref[0] * (IDX_DIM ** -0.5)

    def score_body(c, n):
        at = slab(c, n)
        kk = ki3_ref[at, :]
        for hh in range(N_IDX_HEADS):
            d = _dot_nt(kk, qi3_ref[:, hh * 2 * LANES:(hh + 1) * 2 * LANES])
            term = jnp.maximum(d, 0.0) * w8[hh:hh + 1, :]
            score_ref[at, :] = term if hh == 0 else score_ref[at, :] + term
        score = jnp.where(slab_pos(c, n) < limit, score_ref[at, :] + 0.0, -jnp.inf)
        score_ref[at, :] = score
        sb_ref[at, :] = score.astype(BF16)

    for_chunk_slabs(score_body)

    def count(pred):
        def body(c, acc):
            m = pred(c, score_ref[rows(c), :]).astype(I32)
            return acc + m.reshape(kc // SUBLANES, SUBLANES, tq).sum(axis=0)
        acc = lax.fori_loop(0, nk, body, jnp.zeros((SUBLANES, tq), I32))
        return acc.sum(axis=0, keepdims=True)

    def as_float(t):
        return pltpu.bitcast(t ^ ((t >> 31) & 0x7FFFFFFF), F32)

    def as_bf16_tile(t):
        bits = t ^ ((t >> 15) & 0x7FFF)
        return pltpu.bitcast(jnp.broadcast_to(bits, (PACKED_ROWS, tq)).astype(I16), BF16)

    def count_coarse(t):
        cand = as_bf16_tile(t)
        one = jnp.ones((PACKED_ROWS, tq), I16)
        zero = jnp.zeros((PACKED_ROWS, tq), I16)

        def body(c, acc):
            s = sb_ref[rows(c), :]
            for r in range(kc // PACKED_ROWS):
                hit = s[r * PACKED_ROWS:(r + 1) * PACKED_ROWS, :] >= cand
                acc = acc + jnp.where(hit, one, zero)
            return acc

        acc = lax.fori_loop(0, nk, body, zero)
        return acc.astype(I32).sum(axis=0, keepdims=True)

    def select():
        c_nonneg = count_coarse(jnp.zeros((1, tq), I32))
        t1 = jnp.where(c_nonneg >= n_sel, 0, I16_MIN).astype(I32)

        def coarse_body(b, t):
            cand = t | (jnp.int32(1) << (14 - b))
            return jnp.where(count_coarse(cand) >= n_sel, cand, t)

        t1 = lax.fori_loop(0, 15, coarse_body, t1)
        live = t1 != I16_MIN
        step = 1 << 16
        image1 = t1 * step + jnp.where(t1 < 0, step - 1, 0)
        base = image1 - (step // 2 + 1)
        fine_bits = 17

        def fine_body(b, carry):
            t, c_at = carry
            cand = t + (jnp.int32(1) << (fine_bits - 1 - b))
            cand_f = as_float(cand)
            cnt = count(lambda c, s: s >= cand_f)
            ok = cnt >= n_sel
            return jnp.where(ok, cand, t), jnp.where(ok, cnt, c_at)

        t_sel, c_ge = lax.fori_loop(0, fine_bits, fine_body, (base, jnp.full((1, tq), n_sel, I32)))
        thr = jnp.where(live, as_float(t_sel), -jnp.inf)

        surplus = jnp.max(jnp.where(live & (c_ge > n_sel), 1, 0))

        def tie_search():
            room = n_sel - count(lambda c, s: s > thr)

            def body(b, jm):
                cand = jm | (jnp.int32(1) << (idx_bits - 1 - b))
                cnt = count(lambda c, s: (s == thr) & (key_pos(c) < cand))
                return jnp.where(cnt < room, cand, jm)
            return lax.fori_loop(0, idx_bits, body, jnp.zeros((1, tq), I32))

        jm = lax.cond(surplus > 0, tie_search, lambda: jnp.full((1, tq), 2 ** idx_bits, I32))
        return thr, jnp.where(live, jm, -1)

    keep_all = (jnp.full((1, tq), -jnp.inf, F32), jnp.full((1, tq), -1, I32))
    thr, jm = lax.cond(max_limit > n_sel, select, lambda: keep_all)

    def bias_body(c, carry):
        s = score_ref[rows(c), :]
        sel = (s > thr) | ((s == thr) & (key_pos(c) <= jm))
        bias_ref[rows(c), :] = jnp.where(sel, 0.0, NEG_BIG)
        return carry

    lax.fori_loop(0, nk, bias_body, 0)

    group = N_HEADS // N_KV_HEADS
    m_ref[...] = jnp.full(m_ref.shape, NEG_BIG, F32)
    l_ref[...] = jnp.zeros(l_ref.shape, F32)
    ot_ref[...] = jnp.zeros(ot_ref.shape, F32)

    def fold(x, op):
        return op(x.reshape(x.shape[0] // SUBLANES, SUBLANES, tq), axis=0)

    def logit_body(c, n):
        at = slab(c, n)
        for hh in range(N_HEADS):
            kv = hh // group
            part = slice(hh * SUBLANES, (hh + 1) * SUBLANES)
            kk = kb_ref[at, kv * HEAD_DIM:(kv + 1) * HEAD_DIM]
            s = _dot_nt(kk, q_ref[:, hh * HEAD_DIM:(hh + 1) * HEAD_DIM]) + bias_ref[at, :]
            s_ref[hh, at, :] = s
            m_ref[part, :] = jnp.maximum(m_ref[part, :], fold(s, jnp.max))

    for_chunk_slabs(logit_body)

    def pv_body(c, n):
        at = slab(c, n)
        for hh in range(N_HEADS):
            kv = hh // group
            part = slice(hh * SUBLANES, (hh + 1) * SUBLANES)
            head = slice(hh * HEAD_DIM, (hh + 1) * HEAD_DIM)
            m = m_ref[part, :].max(axis=0, keepdims=True)
            p = jnp.exp2(s_ref[hh, at, :] - m)
            l_ref[part, :] = l_ref[part, :] + fold(p, jnp.sum)
            vt = jnp.concatenate(
                [vt_ref[c + k, kv * HEAD_DIM:(kv + 1) * HEAD_DIM, :] for k in range(n)], axis=1)
            ot_ref[head, :] = ot_ref[head, :] + _dot(vt, p.astype(BF16))

    for_chunk_slabs(pv_body)
    for hh in range(N_HEADS):
        part = slice(hh * SUBLANES, (hh + 1) * SUBLANES)
        head = slice(hh * HEAD_DIM, (hh + 1) * HEAD_DIM)
        ot_ref[head, :] = ot_ref[head, :] / l_ref[part, :].sum(axis=0, keepdims=True)
    o_ref[...] = ot_ref[...].T.astype(BF16)


def _attn_call(qi3, wit, q, ki3, kb, vt, *, batch, tq, kc, past, n_keys, n_sel):
    tq_total = q.shape[0] // batch
    nq = tq_total // tq
    lp = ki3.shape[1]
    idx_bits = max(1, int(np.ceil(np.log2(lp))))
    d_attn = q.shape[1]
    kernel = functools.partial(_attn_kernel, kc=kc, past=past, n_keys=n_keys, n_sel=n_sel, idx_bits=idx_bits)
    return pl.pallas_call(
        kernel,
        grid=(batch, nq),
        in_specs=[
            pl.BlockSpec((tq, qi3.shape[1]), lambda b, j: (b * nq + j, 0)),
            pl.BlockSpec((1, N_IDX_HEADS, tq), lambda b, j: (b, 0, j)),
            pl.BlockSpec((tq, d_attn), lambda b, j: (b * nq + j, 0)),
            pl.BlockSpec((None, lp, ki3.shape[2]), lambda b, j: (b, 0, 0)),
            pl.BlockSpec((None, lp, kb.shape[2]), lambda b, j: (b, 0, 0)),
            pl.BlockSpec((None, lp // kc, vt.shape[2], kc), lambda b, j: (b, 0, 0, 0)),
        ],
        out_specs=pl.BlockSpec((tq, d_attn), lambda b, j: (b * nq + j, 0)),
        out_shape=jax.ShapeDtypeStruct((batch * tq_total, d_attn), BF16),
        scratch_shapes=[
            pltpu.VMEM((lp, tq), F32),
            pltpu.VMEM((lp, tq), BF16),
            pltpu.VMEM((lp, tq), F32),
            pltpu.VMEM((d_attn, tq), F32),
            pltpu.VMEM((N_HEADS * SUBLANES, tq), F32),
            pltpu.VMEM((N_HEADS * SUBLANES, tq), F32),
            pltpu.VMEM((N_HEADS, lp, tq), F32),
        ],
        compiler_params=pltpu.CompilerParams(
            dimension_semantics=("arbitrary", "arbitrary"), vmem_limit_bytes=VMEM_LIMIT),
        name="attn",
    )(qi3, wit, q, ki3, kb, vt)


def _rope_tables(pos):
    half = ROT_DIM // 2
    inv = ROPE_THETA ** (-np.arange(half, dtype=np.float64) * (2.0 / ROT_DIM))
    ang = np.asarray(pos, np.float64)[:, None] * inv[None, :]
    cos = np.ones((len(pos), HEAD_DIM))
    slo = np.zeros((len(pos), HEAD_DIM))
    shi = np.zeros((len(pos), HEAD_DIM))
    cos[:, :half] = np.cos(ang)
    cos[:, half:ROT_DIM] = np.cos(ang)
    slo[:, :half] = -np.sin(ang)
    shi[:, half:ROT_DIM] = np.sin(ang)
    rep = LANES // HEAD_DIM
    return tuple(jnp.asarray(np.tile(t, (1, rep)), F32) for t in (cos, slo, shi))


def _pad_rows(a, rows):
    return jnp.pad(a, ((0, 0), (0, rows - a.shape[1])) + ((0, 0),) * (a.ndim - 2))


def _layer(x, past, cache, wts, *, tm_ffn, tm_proj, tq, kc):
    (ffn1, projw, woc, woa, ffn2) = wts
    b, t, d = x.shape
    n = b * t
    cache_k, cache_v, cache_ik, conv_prev = cache
    x1 = _ffn_call(x.reshape(n, d), *ffn1, tm=tm_ffn)

    d_conv = conv_prev.shape[-1]
    cprev = jnp.pad(conv_prev, ((0, 0), (SUBLANES - (CONV_W - 1), 0), (0, 0)))
    tables = _rope_tables(past + np.arange(t))
    conv_out, q, k, v, kb, vb, qi3, kiwi, ki3, ulast = _proj_call(
        x1, cprev, projw, tables, tm_proj, t // tm_proj)

    n_keys = past + t
    lp = -(-n_keys // kc) * kc
    kb_all = kb.reshape(b, t, LANES)
    v_all = vb.reshape(b, t, LANES)
    ki3_all = ki3.reshape(b, t, 2 * LANES)
    if past:
        ck = cache_k.reshape(b, past, LANES).astype(BF16)
        cv = cache_v.reshape(b, past, LANES).astype(BF16)
        cik = jnp.pad(cache_ik, ((0, 0), (0, 0), (0, LANES - IDX_DIM))).reshape(b * past, LANES)
        cik3 = _stack_keys_call(cik, past).reshape(b, past, 2 * LANES)
        kb_all = jnp.concatenate([ck, kb_all], axis=1)
        v_all = jnp.concatenate([cv, v_all], axis=1)
        ki3_all = jnp.concatenate([cik3, ki3_all], axis=1)
    kb_all = _pad_rows(kb_all, lp)
    ki3_all = _pad_rows(ki3_all, lp)
    vt = jnp.swapaxes(_pad_rows(v_all, lp).reshape(b, lp // kc, kc, LANES), 2, 3)

    tqp = -(-t // tq) * tq
    wi = kiwi[:, IDX_DIM:IDX_DIM + N_IDX_HEADS].reshape(b, t, N_IDX_HEADS)
    wit = jnp.swapaxes(_pad_rows(wi, tqp), 1, 2)
    if tqp == t:
        padq = lambda a: a
    else:
        padq = lambda a: _pad_rows(a.reshape(b, t, a.shape[-1]), tqp).reshape(b * tqp, a.shape[-1])
    n_sel = max(1, min(TOP_K, n_keys // 4))
    attn = _attn_call(padq(qi3), wit, padq(q), ki3_all, kb_all, vt,
                      batch=b, tq=tq, kc=kc, past=past, n_keys=n_keys, n_sel=n_sel)
    if tqp != t:
        attn = attn.reshape(b, tqp, -1)[:, :t].reshape(n, -1)

    y = _out_ffn_call(x1, conv_out, attn, woc, woa, *ffn2, tm=tm_ffn)

    tiles = t // tm_proj
    conv_state = ulast.reshape(b, tiles, SUBLANES, d_conv)[:, -1, SUBLANES - (CONV_W - 1):]
    return (y.reshape(b, t, d),
            k.reshape(b, t, N_KV_HEADS, HEAD_DIM),
            v.reshape(b, t, N_KV_HEADS, HEAD_DIM),
            kiwi[:, :IDX_DIM].reshape(b, t, IDX_DIM),
            conv_state)


def _prep_weights(l, ffn1_norm, ffn1_w_gate, ffn1_w_up, ffn1_w_down, mix_norm, w_in, conv_w,
                  q_norm, k_norm, idx_k_norm, w_out, ffn2_norm, ffn2_w_gate, ffn2_w_up, ffn2_w_down):
    d_conv = conv_w.shape[-1]
    n_main = 3 * d_conv + N_HEADS * HEAD_DIM + 2 * N_KV_HEADS * HEAD_DIM
    row = lambda g: g[l][None, :]
    ffn1 = (row(ffn1_norm), ffn1_w_gate[l].astype(BF16), ffn1_w_up[l].astype(BF16), ffn1_w_down[l].astype(BF16))
    ffn2 = (row(ffn2_norm), ffn2_w_gate[l].astype(BF16), ffn2_w_up[l].astype(BF16), ffn2_w_down[l].astype(BF16))
    w = w_in[l]
    wmain = w[:, :n_main].astype(BF16)
    widx = w[:, n_main:]
    widx = jnp.pad(widx, ((0, 0), (0, -widx.shape[1] % LANES)))
    wih = widx.astype(BF16)
    wil = (widx - wih.astype(F32)).astype(BF16)
    gq = jnp.tile(q_norm[l], LANES // HEAD_DIM)[None, :]
    gk = jnp.tile(k_norm[l], LANES // HEAD_DIM)[None, :]
    gik = jnp.pad(idx_k_norm[l], (0, LANES - IDX_DIM))[None, :]
    lane = np.arange(LANES)
    bd = jnp.asarray(lane[:, None] // HEAD_DIM == lane[None, :] // HEAD_DIM, BF16)
    projw = (row(mix_norm), wmain, wih, wil, conv_w[l], gq, gk, gik, bd)
    wo = w_out[l].astype(BF16)
    return (ffn1, projw, wo[:d_conv], wo[d_conv:], ffn2)


def kernel(x_prompt, x_sample, cache_k, cache_v, cache_idx_k, state_conv, ffn1_norm, ffn1_w_gate, ffn1_w_up, ffn1_w_down, mix_norm, w_in, conv_w, q_norm, k_norm, idx_k_norm, w_out, ffn2_norm, ffn2_w_gate, ffn2_w_up, ffn2_w_down):
    depth = w_in.shape[0]
    bp, tp, _ = x_prompt.shape
    bs, ts, _ = x_sample.shape
    past = cache_k.shape[2]
    d_conv = conv_w.shape[-1]
    hp, hs = x_prompt, x_sample
    outs_p, outs_s = [], []
    for l in range(depth):
        wts = _prep_weights(l, ffn1_norm, ffn1_w_gate, ffn1_w_up, ffn1_w_down, mix_norm, w_in, conv_w,
                            q_norm, k_norm, idx_k_norm, w_out, ffn2_norm, ffn2_w_gate, ffn2_w_up, ffn2_w_down)
        empty = (None, None, None, jnp.zeros((bp, CONV_W - 1, d_conv), F32))
        hp, *rest_p = _layer(hp, 0, empty, wts, tm_ffn=512, tm_proj=512, tq=256, kc=256)
        cache = (cache_k[l], cache_v[l], cache_idx_k[l], state_conv[l])
        hs, *rest_s = _layer(hs, past, cache, wts, tm_ffn=bs * ts, tm_proj=ts, tq=128, kc=256)
        outs_p.append(rest_p)
        outs_s.append(rest_s)
    stack = lambda outs, i: jnp.stack([o[i] for o in outs])
    return (hp, hs,
            stack(outs_p, 0), stack(outs_p, 1), stack(outs_p, 2), stack(outs_p, 3),
            stack(outs_s, 0), stack(outs_s, 1), stack(outs_s, 2), stack(outs_s, 3))
```
